```python
import jax, jax.numpy as jnp
from jax import lax
import numpy as np

D_MODEL = 2048
BATCH = 2
SEQ = 8192
DEPTH = 2

GRID_W = 64
CTX_LEN = 256
D_LRU = 1024
N_LRU_HEADS = 4
LRU_HEAD_DIM = D_LRU // N_LRU_HEADS
RG_C = 8.0
CONV_SHORT = 4
SHORT_PAD_L = 2
SHORT_PAD_R = 1
D_CONV = 1024
CONV_K = 31
D_MIX = D_LRU + D_CONV
D_IN = 2 * D_LRU + 2 * D_CONV
D_FF = 5632
N_MOD = 9
ALPHA = (2 * DEPTH) ** 0.25
BETA = (8 * DEPTH) ** -0.25
ADA_SCALE = 0.5
EPS = 1e-6

kernel_name = 'hybrid_rglru_conformer_dit_block'


def _layernorm(x, g, b):
    xf = x.astype(jnp.float32)
    mu = jnp.mean(xf, axis=-1, keepdims=True)
    var = jnp.mean(jnp.square(xf - mu), axis=-1, keepdims=True)
    y = (xf - mu) * lax.rsqrt(var + EPS) * g.astype(jnp.float32) + b.astype(jnp.float32)
    return y.astype(x.dtype)


def _modulate(s, shift, scale):
    return s * (1 + scale) + shift


def _swiglu(h, w_in, w_out):
    gate, up = jnp.split(h @ w_in, 2, axis=-1)
    return (jax.nn.silu(gate) * up) @ w_out


def _half_ffn(s, shift, scale, gate, w1, w2, g, b):
    return _layernorm(ALPHA * s + 0.5 * gate * _swiglu(_modulate(s, shift, scale), w1, w2), g, b)


def _dwconv(x, w, b, pad_left, pad_right):
    y = lax.conv_general_dilated(
        x, w.astype(x.dtype)[:, None, :], window_strides=(1,),
        padding=[(pad_left, pad_right)], dimension_numbers=('NWC', 'WIO', 'NWC'),
        feature_group_count=x.shape[-1])
    return y + b


def _lin_combine(left, right):
    a1, b1 = left
    a2, b2 = right
    return a1 * a2, a2 * b1 + b2


def _rglru(xc, w_r, b_r, w_i, b_i, lam, h0, reverse):
    bsz, t, c = xc.shape
    xf = xc.astype(jnp.float32)
    xh = xf.reshape(bsz, t, N_LRU_HEADS, LRU_HEAD_DIM)
    r = jax.nn.sigmoid(jnp.einsum('bthi,hij->bthj', xh, w_r.astype(jnp.float32)).reshape(bsz, t, c)
                       + b_r.astype(jnp.float32))
    i = jax.nn.sigmoid(jnp.einsum('bthi,hij->bthj', xh, w_i.astype(jnp.float32)).reshape(bsz, t, c)
                       + b_i.astype(jnp.float32))
    log_a = -RG_C * r * jax.nn.softplus(-lam.astype(jnp.float32))
    a = jnp.exp(log_a)
    b = jnp.sqrt(-jnp.expm1(2.0 * log_a)) * (i * xf)
    if h0 is not None:
        idx = t - 1 if reverse else 0
        b = b.at[:, idx].add(a[:, idx] * h0)
    _, h = lax.associative_scan(_lin_combine, (a, b), axis=1, reverse=reverse)
    return h


def _conv_module(v, gate, w, b, g, bb, n_seg):
    bsz, t, c = v.shape
    u = (v * jax.nn.sigmoid(gate)).reshape(bsz * n_seg, t // n_seg, c)
    u = _dwconv(u, w, b, CONV_K // 2, CONV_K // 2).reshape(bsz, t, c)
    return jax.nn.silu(_layernorm(u, g, bb))


def _mixer(h_lat, h_ctx, rows, w_in, conv4_w, conv4_b, w_rg, b_rg, w_ig, b_ig, lam,
           conv31_w, conv31_b, cln_g, cln_b, w_out, b_out, ctx_out):
    splits = [D_LRU, 2 * D_LRU, 2 * D_LRU + D_CONV]
    xr_l, gr_l, cv_l, cg_l = jnp.split(h_lat @ w_in, splits, axis=-1)
    if ctx_out:
        xr_c, gr_c, cv_c, cg_c = jnp.split(h_ctx @ w_in, splits, axis=-1)
    else:
        xr_c = h_ctx @ w_in[:, :D_LRU]
    xr_l = _dwconv(xr_l, conv4_w, conv4_b, SHORT_PAD_L, SHORT_PAD_R)
    xr_c = _dwconv(xr_c, conv4_w, conv4_b, SHORT_PAD_L, SHORT_PAD_R)
    rec_l = []
    rec_c = []
    for d, rev in ((0, False), (1, True)):
        h_c = _rglru(xr_c, w_rg[d], b_rg[d], w_ig[d], b_ig[d], lam[d], None, rev)
        h0 = h_c[:, 0] if rev else h_c[:, -1]
        h_l = _rglru(xr_l, w_rg[d], b_rg[d], w_ig[d], b_ig[d], lam[d], h0, rev)
        rec_l.append(h_l)
        rec_c.append(h_c)
    y_rec_l = (rec_l[0] + rec_l[1]).astype(h_lat.dtype) * jax.nn.gelu(gr_l)
    y_conv_l = _conv_module(cv_l, cg_l, conv31_w, conv31_b, cln_g, cln_b, rows)
    y_lat = jnp.concatenate([y_rec_l, y_conv_l], axis=-1) @ w_out + b_out
    if not ctx_out:
        return y_lat, None
    y_rec_c = (rec_c[0] + rec_c[1]).astype(h_ctx.dtype) * jax.nn.gelu(gr_c)
    y_conv_c = _conv_module(cv_c, cg_c, conv31_w, conv31_b, cln_g, cln_b, 1)
    y_ctx = jnp.concatenate([y_rec_c, y_conv_c], axis=-1) @ w_out + b_out
    return y_lat, y_ctx


def setup_inputs(seed: int = 0) -> dict:
    key = jax.random.key(seed)
    ks = jax.random.split(key, 32)

    def nrm(k, shape, scale):
        return jax.random.normal(k, shape, jnp.float32) * scale

    x = nrm(ks[0], (BATCH, SEQ, D_MODEL), 1.0)
    c = nrm(ks[1], (BATCH, D_MODEL), 1.0)
    ctx = nrm(ks[2], (BATCH, CTX_LEN, D_MODEL), 1.0)
    c_ctx = nrm(ks[3], (D_MODEL,), 1.0)
    w_ada = nrm(ks[4], (DEPTH, D_MODEL, N_MOD * D_MODEL), ADA_SCALE * D_MODEL ** -0.5)
    b_ada = nrm(ks[5], (DEPTH, N_MOD * D_MODEL), 0.02)
    ln_g = 1.0 + nrm(ks[6], (DEPTH, 3, D_MODEL), 0.02)
    ln_b = nrm(ks[7], (DEPTH, 3, D_MODEL), 0.02)
    ff1_in = nrm(ks[8], (DEPTH, D_MODEL, 2 * D_FF), D_MODEL ** -0.5)
    ff1_out = nrm(ks[9], (DEPTH, D_FF, D_MODEL), BETA * D_FF ** -0.5)
    ff2_in = nrm(ks[10], (DEPTH, D_MODEL, 2 * D_FF), D_MODEL ** -0.5)
    ff2_out = nrm(ks[11], (DEPTH, D_FF, D_MODEL), BETA * D_FF ** -0.5)
    w_in = nrm(ks[12], (DEPTH, D_MODEL, D_IN), D_MODEL ** -0.5)
    conv4_w = nrm(ks[13], (DEPTH, CONV_SHORT, D_LRU), CONV_SHORT ** -0.5)
    conv4_b = nrm(ks[14], (DEPTH, D_LRU), 0.02)
    w_rg = nrm(ks[15], (DEPTH, 2, N_LRU_HEADS, LRU_HEAD_DIM, LRU_HEAD_DIM), LRU_HEAD_DIM ** -0.5)
    b_rg = nrm(ks[16], (DEPTH, 2, D_LRU), 0.02)
    w_ig = nrm(ks[17], (DEPTH, 2, N_LRU_HEADS, LRU_HEAD_DIM, LRU_HEAD_DIM), LRU_HEAD_DIM ** -0.5)
    b_ig = nrm(ks[18], (DEPTH, 2, D_LRU), 0.02)
    a_c = jax.random.uniform(ks[19], (DEPTH, 2, D_LRU), jnp.float32, minval=0.9, maxval=0.999)
    a_base = a_c ** (1.0 / RG_C)
    lam = jnp.log(a_base) - jnp.log1p(-a_base)
    conv31_w = nrm(ks[20], (DEPTH, CONV_K, D_CONV), CONV_K ** -0.5)
    conv31_b = nrm(ks[21], (DEPTH, D_CONV), 0.02)
    cln_g = 1.0 + nrm(ks[22], (DEPTH, D_CONV), 0.02)
    cln_b = nrm(ks[23], (DEPTH, D_CONV), 0.02)
    w_out = nrm(ks[24], (DEPTH, D_MIX, D_MODEL), BETA * D_MIX ** -0.5)
    b_out = nrm(ks[25], (DEPTH, D_MODEL), 0.02)
    return {'x': x, 'c': c, 'ctx': ctx, 'c_ctx': c_ctx, 'w_ada': w_ada, 'b_ada': b_ada,
            'ln_g': ln_g, 'ln_b': ln_b, 'ff1_in': ff1_in, 'ff1_out': ff1_out,
            'ff2_in': ff2_in, 'ff2_out': ff2_out, 'w_in': w_in, 'conv4_w': conv4_w,
            'conv4_b': conv4_b, 'w_rg': w_rg, 'b_rg': b_rg, 'w_ig': w_ig, 'b_ig': b_ig,
            'lam': lam, 'conv31_w': conv31_w, 'conv31_b': conv31_b, 'cln_g': cln_g,
            'cln_b': cln_b, 'w_out': w_out, 'b_out': b_out}


def reference(x, c, ctx, c_ctx, w_ada, b_ada, ln_g, ln_b, ff1_in, ff1_out, ff2_in, ff2_out,
              w_in, conv4_w, conv4_b, w_rg, b_rg, w_ig, b_ig, lam, conv31_w, conv31_b,
              cln_g, cln_b, w_out, b_out):
    rows = x.shape[1] // GRID_W
    for l in range(DEPTH):
        last = l == DEPTH - 1
        m = jnp.split((jax.nn.silu(c) @ w_ada[l] + b_ada[l])[:, None, :], N_MOD, axis=-1)
        mc = jnp.split((jax.nn.silu(c_ctx) @ w_ada[l] + b_ada[l])[None, None, :], N_MOD, axis=-1)
        x = _half_ffn(x, m[0], m[1], m[2], ff1_in[l], ff1_out[l], ln_g[l, 0], ln_b[l, 0])
        ctx = _half_ffn(ctx, mc[0], mc[1], mc[2], ff1_in[l], ff1_out[l], ln_g[l, 0], ln_b[l, 0])
        y_lat, y_ctx = _mixer(_modulate(x, m[3], m[4]), _modulate(ctx, mc[3], mc[4]), rows,
                              w_in[l], conv4_w[l], conv4_b[l], w_rg[l], b_rg[l], w_ig[l], b_ig[l],
                              lam[l], conv31_w[l], conv31_b[l], cln_g[l], cln_b[l], w_out[l],
                              b_out[l], not last)
        x = _layernorm(ALPHA * x + m[5] * y_lat, ln_g[l, 1], ln_b[l, 1])
        if not last:
            ctx = _layernorm(ALPHA * ctx + mc[5] * y_ctx, ln_g[l, 1], ln_b[l, 1])
            ctx = _half_ffn(ctx, mc[6], mc[7], mc[8], ff2_in[l], ff2_out[l], ln_g[l, 2], ln_b[l, 2])
        x = _half_ffn(x, m[6], m[7], m[8], ff2_in[l], ff2_out[l], ln_g[l, 2], ln_b[l, 2])
    return x
```

```python
import functools

import jax
import jax.numpy as jnp
from jax import lax
from jax.experimental import pallas as pl
from jax.experimental.pallas import tpu as pltpu

GRID_W = 64
N_LRU_HEADS = 4
RG_C = 8.0
CONV_SHORT = 4
SHORT_PAD_L = 2
CONV_K = 31
N_MOD = 9
ADA_ROWS = 8
EPS = 1e-6

SUBLANES = 8
VMEM_LIMIT = 56 * 1024 * 1024

F32 = jnp.float32
BF16 = jnp.bfloat16


def _params(semantics):
    return pltpu.CompilerParams(dimension_semantics=semantics, vmem_limit_bytes=VMEM_LIMIT)


def _sigmoid(x):
    return 1.0 / (1.0 + jnp.exp(-x))


def _layernorm(z, g, b):
    mu = jnp.mean(z, axis=-1, keepdims=True)
    d = z - mu
    var = jnp.mean(d * d, axis=-1, keepdims=True)
    return d * lax.rsqrt(var + EPS) * g + b


def _gelu_tanh(x):
    return 0.5 * x * (1.0 + jnp.tanh(0.7978845608028654 * (x + 0.044715 * (x * x * x))))


def _ada_kernel(c_ref, w_ref, b_ref, o_ref):
    c = c_ref[...]
    a = c * _sigmoid(c)
    o_ref[0] = jnp.dot(a, w_ref[0], preferred_element_type=F32) + b_ref[0]


def _ada(cond, w_ada, b_ada, tn=1024):
    depth, d, n = w_ada.shape
    return pl.pallas_call(
        _ada_kernel,
        grid=(depth, n // tn),
        in_specs=[
            pl.BlockSpec((ADA_ROWS, d), lambda l, j: (0, 0)),
            pl.BlockSpec((1, d, tn), lambda l, j: (l, 0, j)),
            pl.BlockSpec((1, 1, tn), lambda l, j: (l, 0, j)),
        ],
        out_specs=pl.BlockSpec((1, ADA_ROWS, tn), lambda l, j: (l, 0, j)),
        out_shape=jax.ShapeDtypeStruct((depth, ADA_ROWS, n), F32),
        compiler_params=_params(("parallel", "parallel")),
        name="ada",
    )(cond, w_ada, b_ada.reshape(depth, 1, n))


def _ffn_kernel(s_ref, sh_ref, sc_ref, gt_ref, w1g_ref, w1u_ref, w2_ref, g_ref, b_ref,
                o_ref, h_ref, *, alpha, n_chunks):
    j = pl.program_id(1)

    @pl.when(j == 0)
    def _():
        h_ref[...] = (s_ref[...] * (1.0 + sc_ref[0]) + sh_ref[0]).astype(BF16)

    h = h_ref[...]
    gate = jnp.dot(h, w1g_ref[...], preferred_element_type=F32)
    up = jnp.dot(h, w1u_ref[...], preferred_element_type=F32)
    act = (gate * _sigmoid(gate) * up).astype(BF16)
    y = jnp.dot(act, w2_ref[...], preferred_element_type=F32)

    @pl.when(j == 0)
    def _():
        o_ref[...] = y

    @pl.when(j > 0)
    def _():
        o_ref[...] += y

    @pl.when(j == n_chunks - 1)
    def _():
        z = alpha * s_ref[...] + 0.5 * gt_ref[0] * o_ref[...]
        o_ref[...] = _layernorm(z, g_ref[...], b_ref[...])


def _half_ffn(s, shift, scale, gate, w1, w2, g, b, *, alpha, tm=512, tf=512):
    n, d = s.shape
    d_ff = w2.shape[0]
    groups = shift.shape[0]
    tiles_per_group = n // groups // tm
    n_chunks = d_ff // tf
    mod_spec = pl.BlockSpec((1, 1, d), lambda i, j: (i // tiles_per_group, 0, 0))
    vec_spec = pl.BlockSpec((1, d), lambda i, j: (0, 0))
    return pl.pallas_call(
        functools.partial(_ffn_kernel, alpha=alpha, n_chunks=n_chunks),
        grid=(n // tm, n_chunks),
        in_specs=[
            pl.BlockSpec((tm, d), lambda i, j: (i, 0)),
            mod_spec, mod_spec, mod_spec,
            pl.BlockSpec((d, tf), lambda i, j: (0, j)),
            pl.BlockSpec((d, tf), lambda i, j: (0, j + n_chunks)),
            pl.BlockSpec((tf, d), lambda i, j: (j, 0)),
            vec_spec, vec_spec,
        ],
        out_specs=pl.BlockSpec((tm, d), lambda i, j: (i, 0)),
        out_shape=jax.ShapeDtypeStruct((n, d), F32),
        scratch_shapes=[pltpu.VMEM((tm, d), BF16)],
        compiler_params=_params(("parallel", "arbitrary")),
        name="half_ffn",
    )(s, shift, scale, gate, w1, w1, w2, g.reshape(1, d), b.reshape(1, d))


def _mix_in_kernel(x_ref, sh_ref, sc_ref, w_ref, *rest, seg, conv_out):
    if conv_out:
        c31w_ref, c31b_ref, cg_ref, cb_ref, xr_ref, gl_ref, yc_ref, upad_ref = rest
    else:
        (xr_ref,) = rest
    d_lru = xr_ref.shape[1]
    h = (x_ref[...] * (1.0 + sc_ref[0]) + sh_ref[0]).astype(BF16)
    xr_ref[...] = jnp.dot(h, w_ref[:, 0:d_lru], preferred_element_type=F32)
    if not conv_out:
        return
    d_conv = yc_ref.shape[1]
    gr = jnp.dot(h, w_ref[:, d_lru:2 * d_lru], preferred_element_type=F32)
    gl_ref[...] = _gelu_tanh(gr)
    cv = jnp.dot(h, w_ref[:, 2 * d_lru:2 * d_lru + d_conv], preferred_element_type=F32)
    cgate = jnp.dot(h, w_ref[:, 2 * d_lru + d_conv:], preferred_element_type=F32)
    u = cv * _sigmoid(cgate)

    tm = u.shape[0]
    nseg = tm // seg
    half = CONV_K // 2
    lead = 2 * SUBLANES
    tail = upad_ref.shape[1] - lead - seg
    upad_ref[:, 0:lead, :] = jnp.zeros((nseg, lead, d_conv), F32)
    upad_ref[:, lead + seg:, :] = jnp.zeros((nseg, tail, d_conv), F32)
    upad_ref[:, lead:lead + seg, :] = u.reshape(nseg, seg, d_conv)
    acc = jnp.zeros((nseg, seg, d_conv), F32) + c31b_ref[...]
    for k in range(CONV_K):
        off = lead - half + k
        acc = acc + c31w_ref[k:k + 1, :] * upad_ref[:, off:off + seg, :]
    y = _layernorm(acc.reshape(tm, d_conv), cg_ref[...], cb_ref[...])
    yc_ref[...] = (y * _sigmoid(y)).astype(BF16)


def _mix_in(x, shift, scale, w_in, c31w, c31b, cg, cb, *, seg, d_lru, conv_out, tm=512):
    n, d = x.shape
    d_in = w_in.shape[1]
    d_conv = (d_in - 2 * d_lru) // 2
    groups = shift.shape[0]
    tiles_per_group = n // groups // tm
    mod_spec = pl.BlockSpec((1, 1, d), lambda i: (i // tiles_per_group, 0, 0))
    row_lru = pl.BlockSpec((tm, d_lru), lambda i: (i, 0))
    in_specs = [pl.BlockSpec((tm, d), lambda i: (i, 0)), mod_spec, mod_spec]
    args = [x, shift, scale]
    if conv_out:
        in_specs += [
            pl.BlockSpec((d, d_in), lambda i: (0, 0)),
            pl.BlockSpec((CONV_K, d_conv), lambda i: (0, 0)),
            pl.BlockSpec((1, d_conv), lambda i: (0, 0)),
            pl.BlockSpec((1, d_conv), lambda i: (0, 0)),
            pl.BlockSpec((1, d_conv), lambda i: (0, 0)),
        ]
        args += [w_in, c31w, c31b.reshape(1, d_conv), cg.reshape(1, d_conv), cb.reshape(1, d_conv)]
        out_specs = [row_lru, row_lru, pl.BlockSpec((tm, d_conv), lambda i: (i, 0))]
        out_shape = [jax.ShapeDtypeStruct((n, d_lru), F32), jax.ShapeDtypeStruct((n, d_lru), F32),
                     jax.ShapeDtypeStruct((n, d_conv), BF16)]
        pad_rows = 2 * SUBLANES + seg + 2 * SUBLANES
        scratch = [pltpu.VMEM((tm // seg, pad_rows, d_conv), F32)]
    else:
        in_specs += [pl.BlockSpec((d, d_lru), lambda i: (0, 0))]
        args += [w_in]
        out_specs = [row_lru]
        out_shape = [jax.ShapeDtypeStruct((n, d_lru), F32)]
        scratch = []
    return pl.pallas_call(
        functools.partial(_mix_in_kernel, seg=seg, conv_out=conv_out),
        grid=(n // tm,),
        in_specs=in_specs,
        out_specs=out_specs,
        out_shape=out_shape,
        scratch_shapes=scratch,
        compiler_params=_params(("parallel",)),
        name="mix_in",
    )(*args)


def _scan_kernel(xr_ref, xp_ref, xn_ref, c4w_ref, c4b_ref, wr_ref, br_ref, wi_ref, bi_ref,
                 lam_ref, h0_ref, *rest, reverse, tiles_per_seq, n_tiles, fuse_out):
    if fuse_out:
        hb_ref, gl_ref, o_ref, ext_ref, a_ref, b_ref, carry_ref, hs_ref = rest
    else:
        o_ref, ext_ref, a_ref, b_ref, carry_ref = rest
        hs_ref = o_ref
    i = pl.program_id(0)
    t = (n_tiles - 1 - i) if reverse else i
    p = t % tiles_per_seq
    first = p == 0
    last = p == tiles_per_seq - 1
    tm, c = xr_ref.shape
    hd = c // N_LRU_HEADS

    ext_ref[0:SUBLANES, :] = jnp.where(first, 0.0, xp_ref[...])
    ext_ref[SUBLANES:SUBLANES + tm, :] = xr_ref[...]
    ext_ref[SUBLANES + tm:, :] = jnp.where(last, 0.0, xn_ref[...])
    xc = jnp.zeros((tm, c), F32) + c4b_ref[...]
    for k in range(CONV_SHORT):
        off = SUBLANES - SHORT_PAD_L + k
        xc = xc + c4w_ref[k:k + 1, :] * ext_ref[off:off + tm, :]

    xcb = xc.astype(BF16)
    lam = lam_ref[...]
    softplus_neg_lam = jnp.maximum(-lam, 0.0) + jnp.log(1.0 + jnp.exp(-jnp.abs(lam)))
    coef = -RG_C * softplus_neg_lam
    for hh in range(N_LRU_HEADS):
        cs = slice(hh * hd, (hh + 1) * hd)
        xh = xcb[:, cs]
        r = _sigmoid(jnp.dot(xh, wr_ref[hh], preferred_element_type=F32) + br_ref[:, cs])
        ig = _sigmoid(jnp.dot(xh, wi_ref[hh], preferred_element_type=F32) + bi_ref[:, cs])
        log_a = coef[:, cs] * r
        a = jnp.exp(log_a)
        a_ref[:, cs] = a
        b_ref[:, cs] = jnp.sqrt(1.0 - jnp.exp(2.0 * log_a)) * (ig * xc[:, cs])

    start = last if reverse else first

    @pl.when(start)
    def _():
        carry_ref[...] = h0_ref[0]

    n_groups = tm // SUBLANES
    row = lax.broadcasted_iota(jnp.int32, (SUBLANES, c), 0)

    def group(k, carry):
        kk = (n_groups - 1 - k) if reverse else k
        r0 = pl.multiple_of(kk * SUBLANES, SUBLANES)
        a = a_ref[pl.ds(r0, SUBLANES), :]
        b = b_ref[pl.ds(r0, SUBLANES), :]
        for s in (1, 2, 4):
            if reverse:
                shift, keep = SUBLANES - s, row < SUBLANES - s
            else:
                shift, keep = s, row >= s
            a_prev = jnp.where(keep, pltpu.roll(a, shift, 0), 1.0)
            b_prev = jnp.where(keep, pltpu.roll(b, shift, 0), 0.0)
            b = b + a * b_prev
            a = a * a_prev
        h = b + a * carry
        hs_ref[pl.ds(r0, SUBLANES), :] = h
        return h[0:1, :] if reverse else h[SUBLANES - 1:SUBLANES, :]

    carry_ref[...] = lax.fori_loop(0, n_groups, group, carry_ref[...])

    if fuse_out:
        o_ref[...] = ((hs_ref[...] + hb_ref[...]) * gl_ref[...]).astype(BF16)


def _scan(xr, c4w, c4b, wr, br, wi, bi, lam, h0, *, reverse, seq_len, other=None, gelu=None, tm=512):
    n, c = xr.shape
    tm = min(tm, seq_len)
    n_tiles = n // tm
    tiles_per_seq = seq_len // tm
    hd = c // N_LRU_HEADS
    fuse_out = other is not None
    blocks_per_tile = tm // SUBLANES
    n_blocks = n // SUBLANES

    def tile(i):
        return (n_tiles - 1 - i) if reverse else i

    row_spec = pl.BlockSpec((tm, c), lambda i: (tile(i), 0))
    vec_spec = pl.BlockSpec((1, c), lambda i: (0, 0))
    w_spec = pl.BlockSpec((N_LRU_HEADS, hd, hd), lambda i: (0, 0, 0))
    in_specs = [
        row_spec,
        pl.BlockSpec((SUBLANES, c), lambda i: (jnp.maximum(tile(i) * blocks_per_tile - 1, 0), 0)),
        pl.BlockSpec((SUBLANES, c), lambda i: (jnp.minimum((tile(i) + 1) * blocks_per_tile, n_blocks - 1), 0)),
        pl.BlockSpec((CONV_SHORT, c), lambda i: (0, 0)),
        vec_spec, w_spec, vec_spec, w_spec, vec_spec, vec_spec,
        pl.BlockSpec((1, 1, c), lambda i: (tile(i) // tiles_per_seq, 0, 0)),
    ]
    args = [xr, xr, xr, c4w, c4b.reshape(1, c), wr, br.reshape(1, c), wi, bi.reshape(1, c),
            lam.reshape(1, c), h0]
    scratch = [pltpu.VMEM((tm + 2 * SUBLANES, c), F32), pltpu.VMEM((tm, c), F32),
               pltpu.VMEM((tm, c), F32), pltpu.VMEM((1, c), F32)]
    if fuse_out:
        in_specs += [row_spec, row_spec]
        args += [other, gelu]
        scratch += [pltpu.VMEM((tm, c), F32)]
    return pl.pallas_call(
        functools.partial(_scan_kernel, reverse=reverse, tiles_per_seq=tiles_per_seq,
                          n_tiles=n_tiles, fuse_out=fuse_out),
        grid=(n_tiles,),
        in_specs=in_specs,
        out_specs=row_spec,
        out_shape=jax.ShapeDtypeStruct((n, c), BF16 if fuse_out else F32),
        scratch_shapes=scratch,
        compiler_params=_params(("arbitrary",)),
        name="lru_scan",
    )(*args)


def _mix_out_kernel(x_ref, gt_ref, yr_ref, yc_ref, wa_ref, wb_ref, bo_ref, g_ref, b_ref, o_ref, *, alpha):
    y = jnp.dot(yr_ref[...], wa_ref[...], preferred_element_type=F32)
    y = y + jnp.dot(yc_ref[...], wb_ref[...], preferred_element_type=F32) + bo_ref[...]
    z = alpha * x_ref[...] + gt_ref[0] * y
    o_ref[...] = _layernorm(z, g_ref[...], b_ref[...])


def _mix_out(x, gate, y_rec, y_conv, w_out, b_out, g, b, *, alpha, tm=512):
    n, d = x.shape
    d_lru = y_rec.shape[1]
    d_conv = y_conv.shape[1]
    groups = gate.shape[0]
    tiles_per_group = n // groups // tm
    vec_spec = pl.BlockSpec((1, d), lambda i: (0, 0))
    assert d_lru == d_conv
    return pl.pallas_call(
        functools.partial(_mix_out_kernel, alpha=alpha),
        grid=(n // tm,),
        in_specs=[
            pl.BlockSpec((tm, d), lambda i: (i, 0)),
            pl.BlockSpec((1, 1, d), lambda i: (i // tiles_per_group, 0, 0)),
            pl.BlockSpec((tm, d_lru), lambda i: (i, 0)),
            pl.BlockSpec((tm, d_conv), lambda i: (i, 0)),
            pl.BlockSpec((d_lru, d), lambda i: (0, 0)),
            pl.BlockSpec((d_conv, d), lambda i: (1, 0)),
            vec_spec, vec_spec, vec_spec,
        ],
        out_specs=pl.BlockSpec((tm, d), lambda i: (i, 0)),
        out_shape=jax.ShapeDtypeStruct((n, d), F32),
        compiler_params=_params(("parallel",)),
        name="mix_out",
    )(x, gate, y_rec, y_conv, w_out, w_out, b_out.reshape(1, d), g.reshape(1, d), b.reshape(1, d))


def kernel(x, c, ctx, c_ctx, w_ada, b_ada, ln_g, ln_b, ff1_in, ff1_out, ff2_in, ff2_out, w_in, conv4_w,
           conv4_b, w_rg, b_rg, w_ig, b_ig, lam, conv31_w, conv31_b, cln_g, cln_b, w_out, b_out):
    batch, seq, d = x.shape
    ctx_len = ctx.shape[1]
    depth = w_ada.shape[0]
    d_lru = conv4_w.shape[-1]
    alpha = (2 * depth) ** 0.25

    cond = jnp.concatenate([c, c_ctx[None, :], jnp.zeros((ADA_ROWS - batch - 1, d), F32)], axis=0)
    mods = _ada(cond, w_ada, b_ada).reshape(depth, ADA_ROWS, N_MOD, d)

    xs = x.reshape(batch * seq, d)
    cs = ctx.reshape(batch * ctx_len, d)
    for l in range(depth):
        last = l == depth - 1
        m = [mods[l, 0:batch, k][:, None, :] for k in range(N_MOD)]
        mc = [mods[l, batch:batch + 1, k][:, None, :] for k in range(N_MOD)]
        f1_in, f1_out = ff1_in[l].astype(BF16), ff1_out[l].astype(BF16)
        f2_in, f2_out = ff2_in[l].astype(BF16), ff2_out[l].astype(BF16)
        w_in_l, w_out_l = w_in[l].astype(BF16), w_out[l].astype(BF16)
        wr, wi = w_rg[l].astype(BF16), w_ig[l].astype(BF16)
        conv = (conv31_w[l], conv31_b[l], cln_g[l], cln_b[l])
        lru = lambda d_: (conv4_w[l], conv4_b[l], wr[d_], b_rg[l, d_], wi[d_], b_ig[l, d_], lam[l, d_])

        xs = _half_ffn(xs, m[0], m[1], m[2], f1_in, f1_out, ln_g[l, 0], ln_b[l, 0], alpha=alpha)
        cs = _half_ffn(cs, mc[0], mc[1], mc[2], f1_in, f1_out, ln_g[l, 0], ln_b[l, 0], alpha=alpha)

        zeros_h0 = jnp.zeros((batch, 1, d_lru), F32)
        if not last:
            xr_c, gl_c, yc_c = _mix_in(cs, mc[3], mc[4], w_in_l, *conv, seg=ctx_len, d_lru=d_lru,
                                       conv_out=True)
        else:
            (xr_c,) = _mix_in(cs, mc[3], mc[4], w_in_l[:, :d_lru], *conv, seg=ctx_len, d_lru=d_lru,
                              conv_out=False)
        hb_c = _scan(xr_c, *lru(1), zeros_h0, reverse=True, seq_len=ctx_len)
        h0_b = hb_c.reshape(batch, ctx_len, d_lru)[:, 0:1, :]
        if not last:
            yr_c = _scan(xr_c, *lru(0), zeros_h0, reverse=False, seq_len=ctx_len, other=hb_c, gelu=gl_c)
            hf_c = _scan(xr_c, *lru(0), zeros_h0, reverse=False, seq_len=ctx_len)
        else:
            hf_c = _scan(xr_c, *lru(0), zeros_h0, reverse=False, seq_len=ctx_len)
        h0_f = hf_c.reshape(batch, ctx_len, d_lru)[:, ctx_len - 1:ctx_len, :]

        xr, gl, yc = _mix_in(xs, m[3], m[4], w_in_l, *conv, seg=GRID_W, d_lru=d_lru, conv_out=True)
        hb = _scan(xr, *lru(1), h0_b, reverse=True, seq_len=seq)
        yr = _scan(xr, *lru(0), h0_f, reverse=False, seq_len=seq, other=hb, gelu=gl)
        xs = _mix_out(xs, m[5], yr, yc, w_out_l, b_out[l], ln_g[l, 1], ln_b[l, 1], alpha=alpha)
        if not last:
            cs = _mix_out(cs, mc[5], yr_c, yc_c, w_out_l, b_out[l], ln_g[l, 1], ln_b[l, 1], alpha=alpha)
            cs = _half_ffn(cs, mc[6], mc[7], mc[8], f2_in, f2_out, ln_g[l, 2], ln_b[l, 2], alpha=alpha)

        xs = _half_ffn(xs, m[6], m[7], m[8], f2_in, f2_out, ln_g[l, 2], ln_b[l, 2], alpha=alpha)
    return xs.reshape(batch, seq, d)
```

```python
import functools

import jax
import jax.numpy as jnp
from jax import lax
from jax.experimental import pallas as pl
from jax.experimental.pallas import tpu as pltpu

GRID_W = 64
N_LRU_HEADS = 4
RG_C = 8.0
CONV_SHORT = 4
SHORT_PAD_L = 2
CONV_K = 31
N_MOD = 9
ADA_ROWS = 8
EPS = 1e-6

SUBLANES = 8
LANES = 128
CONV_ROWS = 64
CONV_LANES = LANES
VMEM_LIMIT = 56 * 1024 * 1024

F32 = jnp.float32
BF16 = jnp.bfloat16


def _params(semantics):
    return pltpu.CompilerParams(dimension_semantics=semantics, vmem_limit_bytes=VMEM_LIMIT)


def _sigmoid(x):
    return 0.5 + 0.5 * jnp.tanh(0.5 * x)


def _layernorm(z, g, b):
    mu = jnp.mean(z, axis=-1, keepdims=True)
    d = z - mu
    var = jnp.mean(d * d, axis=-1, keepdims=True)
    return d * lax.rsqrt(var + EPS) * g + b


def _gelu_tanh(x):
    return 0.5 * x * (1.0 + jnp.tanh(0.7978845608028654 * (x + 0.044715 * (x * x * x))))


def _ada_kernel(c_ref, w_ref, b_ref, o_ref):
    c = c_ref[...]
    a = c * _sigmoid(c)
    o_ref[0] = jnp.dot(a, w_ref[0], preferred_element_type=F32) + b_ref[0]


def _ada(cond, w_ada, b_ada, tn=1024):
    depth, d, n = w_ada.shape
    return pl.pallas_call(
        _ada_kernel,
        grid=(depth, n // tn),
        in_specs=[
            pl.BlockSpec((ADA_ROWS, d), lambda l, j: (0, 0)),
            pl.BlockSpec((1, d, tn), lambda l, j: (l, 0, j)),
            pl.BlockSpec((1, 1, tn), lambda l, j: (l, 0, j)),
        ],
        out_specs=pl.BlockSpec((1, ADA_ROWS, tn), lambda l, j: (l, 0, j)),
        out_shape=jax.ShapeDtypeStruct((depth, ADA_ROWS, n), F32),
        compiler_params=_params(("parallel", "parallel")),
        name="ada",
    )(cond, w_ada, b_ada.reshape(depth, 1, n))


def _ffn_kernel(s_ref, sh_ref, sc_ref, gt_ref, w1g_ref, w1u_ref, w2_ref, g_ref, b_ref,
                o_ref, h_ref, *, alpha, n_chunks):
    j = pl.program_id(1)

    @pl.when(j == 0)
    def _():
        h_ref[...] = (s_ref[...] * (1.0 + sc_ref[0]) + sh_ref[0]).astype(BF16)
        o_ref[...] = jnp.zeros_like(o_ref)

    h = h_ref[...]
    gate = jnp.dot(h, w1g_ref[...], preferred_element_type=F32)
    up = jnp.dot(h, w1u_ref[...], preferred_element_type=F32)
    act = (gate * _sigmoid(gate) * up).astype(BF16)
    o_ref[...] += jnp.dot(act, w2_ref[...], preferred_element_type=F32)

    @pl.when(j == n_chunks - 1)
    def _():
        z = alpha * s_ref[...] + 0.5 * gt_ref[0] * o_ref[...]
        o_ref[...] = _layernorm(z, g_ref[...], b_ref[...])


def _half_ffn(s, shift, scale, gate, w1, w2, g, b, *, layer, alpha, tm=512, tf=512):
    n, d = s.shape
    d_ff = w2.shape[1]
    groups = shift.shape[0]
    tiles_per_group = n // groups // tm
    n_chunks = d_ff // tf
    mod_spec = pl.BlockSpec((1, 1, d), lambda i, j: (i // tiles_per_group, 0, 0))
    vec_spec = pl.BlockSpec((1, d), lambda i, j: (0, 0))
    return pl.pallas_call(
        functools.partial(_ffn_kernel, alpha=alpha, n_chunks=n_chunks),
        grid=(n // tm, n_chunks),
        in_specs=[
            pl.BlockSpec((tm, d), lambda i, j: (i, 0)),
            mod_spec, mod_spec, mod_spec,
            pl.BlockSpec((None, d, tf), lambda i, j: (layer, 0, j)),
            pl.BlockSpec((None, d, tf), lambda i, j: (layer, 0, j + n_chunks)),
            pl.BlockSpec((None, tf, d), lambda i, j: (layer, j, 0)),
            vec_spec, vec_spec,
        ],
        out_specs=pl.BlockSpec((tm, d), lambda i, j: (i, 0)),
        out_shape=jax.ShapeDtypeStruct((n, d), F32),
        scratch_shapes=[pltpu.VMEM((tm, d), BF16)],
        compiler_params=_params(("parallel", "arbitrary")),
        name="half_ffn",
    )(s, shift, scale, gate, w1, w1, w2, g.reshape(1, d), b.reshape(1, d))


def _mix_in_kernel(x_ref, sh_ref, sc_ref, w_ref, *rest, seg, conv_out):
    if conv_out:
        c31w_ref, c31b_ref, cg_ref, cb_ref, xr_ref, gl_ref, yc_ref, upad_ref, conv_ref = rest
    else:
        (xr_ref,) = rest
    d_lru = xr_ref.shape[1]
    h = (x_ref[...] * (1.0 + sc_ref[0]) + sh_ref[0]).astype(BF16)
    if not conv_out:
        xr_ref[...] = jnp.dot(h, w_ref[...], preferred_element_type=F32)
        return
    d_conv = yc_ref.shape[1]
    cv = jnp.dot(h, w_ref[:, 2 * d_lru:2 * d_lru + d_conv], preferred_element_type=F32)
    cgate = jnp.dot(h, w_ref[:, 2 * d_lru + d_conv:], preferred_element_type=F32)
    u = cv * _sigmoid(cgate)

    tm = u.shape[0]
    nseg = tm // seg
    half = CONV_K // 2
    lead = 2 * SUBLANES
    tail = upad_ref.shape[1] - lead - seg
    upad_ref[:, 0:lead, :] = jnp.zeros((nseg, lead, d_conv), F32)
    upad_ref[:, lead + seg:, :] = jnp.zeros((nseg, tail, d_conv), F32)
    upad_ref[:, lead:lead + seg, :] = u.reshape(nseg, seg, d_conv)

    xr_ref[...] = jnp.dot(h, w_ref[:, 0:d_lru], preferred_element_type=F32)
    gr = jnp.dot(h, w_ref[:, d_lru:2 * d_lru], preferred_element_type=F32)
    gl_ref[...] = _gelu_tanh(gr)

    offs = [lead - half + k for k in range(CONV_K)]
    slab_rows = CONV_ROWS + lead + tail
    for s in range(nseg):
        for rb in range(seg // CONV_ROWS):
            for cc in range(d_conv // CONV_LANES):
                cs = slice(cc * CONV_LANES, (cc + 1) * CONV_LANES)
                slab = upad_ref[s, rb * CONV_ROWS:rb * CONV_ROWS + slab_rows, cs]
                acc = jnp.broadcast_to(c31b_ref[:, cs], (CONV_ROWS, CONV_LANES))
                for r in range(SUBLANES):
                    rot = slab if r == 0 else pltpu.roll(slab, slab_rows - r, 0)
                    for k in range(CONV_K):
                        if offs[k] % SUBLANES == r:
                            q = offs[k] - r
                            assert q + CONV_ROWS <= slab_rows - r
                            acc = acc + c31w_ref[k:k + 1, cs] * rot[q:q + CONV_ROWS]
                row0 = s * seg + rb * CONV_ROWS
                conv_ref[row0:row0 + CONV_ROWS, cs] = acc
    y = _layernorm(conv_ref[...], cg_ref[...], cb_ref[...])
    yc_ref[...] = (y * _sigmoid(y)).astype(BF16)


def _mix_in(x, shift, scale, w_in, c31w, c31b, cg, cb, *, layer, seg, d_lru, conv_out, tm=512):
    n, d = x.shape
    d_in = w_in.shape[2]
    d_conv = (d_in - 2 * d_lru) // 2
    groups = shift.shape[0]
    tiles_per_group = n // groups // tm
    mod_spec = pl.BlockSpec((1, 1, d), lambda i: (i // tiles_per_group, 0, 0))
    row_lru = pl.BlockSpec((tm, d_lru), lambda i: (i, 0))
    in_specs = [pl.BlockSpec((tm, d), lambda i: (i, 0)), mod_spec, mod_spec]
    args = [x, shift, scale]
    if conv_out:
        in_specs += [
            pl.BlockSpec((None, d, d_in), lambda i: (layer, 0, 0)),
            pl.BlockSpec((CONV_K, d_conv), lambda i: (0, 0)),
            pl.BlockSpec((1, d_conv), lambda i: (0, 0)),
            pl.BlockSpec((1, d_conv), lambda i: (0, 0)),
            pl.BlockSpec((1, d_conv), lambda i: (0, 0)),
        ]
        args += [w_in, c31w, c31b.reshape(1, d_conv), cg.reshape(1, d_conv), cb.reshape(1, d_conv)]
        out_specs = [row_lru, row_lru, pl.BlockSpec((tm, d_conv), lambda i: (i, 0))]
        out_shape = [jax.ShapeDtypeStruct((n, d_lru), F32), jax.ShapeDtypeStruct((n, d_lru), F32),
                     jax.ShapeDtypeStruct((n, d_conv), BF16)]
        pad_rows = 2 * SUBLANES + seg + 2 * SUBLANES
        scratch = [pltpu.VMEM((tm // seg, pad_rows, d_conv), F32), pltpu.VMEM((tm, d_conv), F32)]
    else:
        in_specs += [pl.BlockSpec((None, d, d_lru), lambda i: (layer, 0, 0))]
        args += [w_in]
        out_specs = [row_lru]
        out_shape = [jax.ShapeDtypeStruct((n, d_lru), F32)]
        scratch = []
    return pl.pallas_call(
        functools.partial(_mix_in_kernel, seg=seg, conv_out=conv_out),
        grid=(n // tm,),
        in_specs=in_specs,
        out_specs=out_specs,
        out_shape=out_shape,
        scratch_shapes=scratch,
        compiler_params=_params(("parallel",)),
        name="mix_in",
    )(*args)


def _scan_kernel(xr_ref, xp_ref, xn_ref, c4w_ref, c4b_ref, wr_ref, br_ref, wi_ref, bi_ref,
                 lam_ref, h0_ref, *rest, reverse, tiles_per_seq, n_tiles, fuse_out):
    if fuse_out:
        hb_ref, gl_ref, o_ref, hs_ref, ext_ref, a_ref, b_ref, carry_ref = rest
    else:
        hs_ref, ext_ref, a_ref, b_ref, carry_ref = rest
    i = pl.program_id(0)
    t = (n_tiles - 1 - i) if reverse else i
    p = t % tiles_per_seq
    first = p == 0
    last = p == tiles_per_seq - 1
    tm, c = xr_ref.shape
    hd = c // N_LRU_HEADS

    ext_ref[0:SUBLANES, :] = jnp.where(first, 0.0, xp_ref[...])
    ext_ref[SUBLANES:SUBLANES + tm, :] = xr_ref[...]
    ext_ref[SUBLANES + tm:, :] = jnp.where(last, 0.0, xn_ref[...])
    xc = jnp.zeros((tm, c), F32) + c4b_ref[...]
    for k in range(CONV_SHORT):
        off = SUBLANES - SHORT_PAD_L + k
        xc = xc + c4w_ref[k:k + 1, :] * ext_ref[off:off + tm, :]

    xcb = xc.astype(BF16)
    lam = lam_ref[...]
    softplus_neg_lam = jnp.maximum(-lam, 0.0) + jnp.log(1.0 + jnp.exp(-jnp.abs(lam)))
    coef = -RG_C * softplus_neg_lam
    for hh in range(N_LRU_HEADS):
        cs = slice(hh * hd, (hh + 1) * hd)
        xh = xcb[:, cs]
        r = _sigmoid(jnp.dot(xh, wr_ref[hh], preferred_element_type=F32) + br_ref[:, cs])
        ig = _sigmoid(jnp.dot(xh, wi_ref[hh], preferred_element_type=F32) + bi_ref[:, cs])
        log_a = coef[:, cs] * r
        a = jnp.exp(log_a)
        a_ref[:, cs] = a
        b_ref[:, cs] = jnp.sqrt(1.0 - jnp.exp(2.0 * log_a)) * (ig * xc[:, cs])

    start = last if reverse else first

    @pl.when(start)
    def _():
        carry_ref[...] = h0_ref[0]

    n_groups = tm // SUBLANES
    row = lax.broadcasted_iota(jnp.int32, (SUBLANES, c), 0)

    def group(k, carry):
        kk = (n_groups - 1 - k) if reverse else k
        r0 = pl.multiple_of(kk * SUBLANES, SUBLANES)
        a = a_ref[pl.ds(r0, SUBLANES), :]
        b = b_ref[pl.ds(r0, SUBLANES), :]
        for s in (1, 2, 4):
            if reverse:
                shift, keep = SUBLANES - s, row < SUBLANES - s
            else:
                shift, keep = s, row >= s
            a_prev = jnp.where(keep, pltpu.roll(a, shift, 0), 1.0)
            b_prev = jnp.where(keep, pltpu.roll(b, shift, 0), 0.0)
            b = b + a * b_prev
            a = a * a_prev
        h = b + a * carry
        hs_ref[pl.ds(r0, SUBLANES), :] = h
        return h[0:1, :] if reverse else h[SUBLANES - 1:SUBLANES, :]

    carry_ref[...] = lax.fori_loop(0, n_groups, group, carry_ref[...])

    if fuse_out:
        o_ref[...] = ((hs_ref[...] + hb_ref[...]) * gl_ref[...]).astype(BF16)


def _scan(xr, c4w, c4b, wr, br, wi, bi, lam, h0, *, layer, direction, seq_len, other=None, gelu=None,
          want_states=True, tm=512):
    reverse = direction == 1
    n, c = xr.shape
    tm = min(tm, seq_len)
    n_tiles = n // tm
    tiles_per_seq = seq_len // tm
    hd = c // N_LRU_HEADS
    fuse_out = other is not None
    blocks_per_tile = tm // SUBLANES
    n_blocks = n // SUBLANES

    def tile(i):
        return (n_tiles - 1 - i) if reverse else i

    row_spec = pl.BlockSpec((tm, c), lambda i: (tile(i), 0))
    vec_spec = pl.BlockSpec((1, c), lambda i: (0, 0))
    w_spec = pl.BlockSpec((None, None, N_LRU_HEADS, hd, hd), lambda i: (layer, direction, 0, 0, 0))
    in_specs = [
        row_spec,
        pl.BlockSpec((SUBLANES, c), lambda i: (jnp.maximum(tile(i) * blocks_per_tile - 1, 0), 0)),
        pl.BlockSpec((SUBLANES, c), lambda i: (jnp.minimum((tile(i) + 1) * blocks_per_tile, n_blocks - 1), 0)),
        pl.BlockSpec((CONV_SHORT, c), lambda i: (0, 0)),
        vec_spec, w_spec, vec_spec, w_spec, vec_spec, vec_spec,
        pl.BlockSpec((1, 1, c), lambda i: (tile(i) // tiles_per_seq, 0, 0)),
    ]
    args = [xr, xr, xr, c4w, c4b.reshape(1, c), wr, br.reshape(1, c), wi, bi.reshape(1, c),
            lam.reshape(1, c), h0]
    scratch = [pltpu.VMEM((tm + 2 * SUBLANES, c), F32), pltpu.VMEM((tm, c), F32),
               pltpu.VMEM((tm, c), F32), pltpu.VMEM((1, c), F32)]
    states = jax.ShapeDtypeStruct((n, c), F32)
    fused = jax.ShapeDtypeStruct((n, c), BF16)
    if fuse_out:
        in_specs += [row_spec, row_spec]
        args += [other, gelu]
    if fuse_out and want_states:
        out_specs, out_shape = [row_spec, row_spec], [fused, states]
    elif fuse_out:
        out_specs, out_shape = row_spec, fused
        scratch = [pltpu.VMEM((tm, c), F32)] + scratch
    else:
        out_specs, out_shape = row_spec, states
    return pl.pallas_call(
        functools.partial(_scan_kernel, reverse=reverse, tiles_per_seq=tiles_per_seq,
                          n_tiles=n_tiles, fuse_out=fuse_out),
        grid=(n_tiles,),
        in_specs=in_specs,
        out_specs=out_specs,
        out_shape=out_shape,
        scratch_shapes=scratch,
        compiler_params=_params(("arbitrary",)),
        name="lru_scan",
    )(*args)


def _mix_out_kernel(x_ref, gt_ref, yr_ref, yc_ref, wa_ref, wb_ref, bo_ref, g_ref, b_ref, o_ref, *, alpha):
    y = jnp.dot(yr_ref[...], wa_ref[...], preferred_element_type=F32)
    y = y + jnp.dot(yc_ref[...], wb_ref[...], preferred_element_type=F32) + bo_ref[...]
    z = alpha * x_ref[...] + gt_ref[0] * y
    o_ref[...] = _layernorm(z, g_ref[...], b_ref[...])


def _mix_out(x, gate, y_rec, y_conv, w_out, b_out, g, b, *, layer, alpha, tm=512):
    n, d = x.shape
    d_lru = y_rec.shape[1]
    d_conv = y_conv.shape[1]
    groups = gate.shape[0]
    tiles_per_group = n // groups // tm
    vec_spec = pl.BlockSpec((1, d), lambda i: (0, 0))
    assert d_lru == d_conv
    return pl.pallas_call(
        functools.partial(_mix_out_kernel, alpha=alpha),
        grid=(n // tm,),
        in_specs=[
            pl.BlockSpec((tm, d), lambda i: (i, 0)),
            pl.BlockSpec((1, 1, d), lambda i: (i // tiles_per_group, 0, 0)),
            pl.BlockSpec((tm, d_lru), lambda i: (i, 0)),
            pl.BlockSpec((tm, d_conv), lambda i: (i, 0)),
            pl.BlockSpec((None, d_lru, d), lambda i: (layer, 0, 0)),
            pl.BlockSpec((None, d_conv, d), lambda i: (layer, 1, 0)),
            vec_spec, vec_spec, vec_spec,
        ],
        out_specs=pl.BlockSpec((tm, d), lambda i: (i, 0)),
        out_shape=jax.ShapeDtypeStruct((n, d), F32),
        compiler_params=_params(("parallel",)),
        name="mix_out",
    )(x, gate, y_rec, y_conv, w_out, w_out, b_out.reshape(1, d), g.reshape(1, d), b.reshape(1, d))


def kernel(x, c, ctx, c_ctx, w_ada, b_ada, ln_g, ln_b, ff1_in, ff1_out, ff2_in, ff2_out, w_in, conv4_w,
           conv4_b, w_rg, b_rg, w_ig, b_ig, lam, conv31_w, conv31_b, cln_g, cln_b, w_out, b_out):
    batch, seq, d = x.shape
    ctx_len = ctx.shape[1]
    depth = w_ada.shape[0]
    d_lru = conv4_w.shape[-1]
    alpha = (2 * depth) ** 0.25

    cond = jnp.concatenate([c, c_ctx[None, :], jnp.zeros((ADA_ROWS - batch - 1, d), F32)], axis=0)
    mods = _ada(cond, w_ada, b_ada).reshape(depth, ADA_ROWS, N_MOD, d)

    f1_in, f1_out = ff1_in.astype(BF16), ff1_out.astype(BF16)
    f2_in, f2_out = ff2_in.astype(BF16), ff2_out.astype(BF16)
    w_in_b, w_out_b = w_in.astype(BF16), w_out.astype(BF16)
    wr, wi = w_rg.astype(BF16), w_ig.astype(BF16)

    xs = x.reshape(batch * seq, d)
    cs = ctx.reshape(batch * ctx_len, d)
    zeros_h0 = jnp.zeros((batch, 1, d_lru), F32)
    for l in range(depth):
        last = l == depth - 1
        m = [mods[l, 0:batch, k][:, None, :] for k in range(N_MOD)]
        mc = [mods[l, batch:batch + 1, k][:, None, :] for k in range(N_MOD)]
        conv = (conv31_w[l], conv31_b[l], cln_g[l], cln_b[l])

        def scan(xr_, h0_, direction, seq_len, **kw):
            return _scan(xr_, conv4_w[l], conv4_b[l], wr, b_rg[l, direction], wi, b_ig[l, direction],
                         lam[l, direction], h0_, layer=l, direction=direction, seq_len=seq_len, **kw)

        ln1 = (ln_g[l, 0], ln_b[l, 0])
        xs = _half_ffn(xs, m[0], m[1], m[2], f1_in, f1_out, *ln1, layer=l, alpha=alpha)
        cs = _half_ffn(cs, mc[0], mc[1], mc[2], f1_in, f1_out, *ln1, layer=l, alpha=alpha)

        if not last:
            xr_c, gl_c, yc_c = _mix_in(cs, mc[3], mc[4], w_in_b, *conv, layer=l, seg=ctx_len, d_lru=d_lru,
                                       conv_out=True)
        else:
            (xr_c,) = _mix_in(cs, mc[3], mc[4], w_in_b, *conv, layer=l, seg=ctx_len, d_lru=d_lru,
                              conv_out=False)
        hb_c = scan(xr_c, zeros_h0, 1, ctx_len)
        if not last:
            yr_c, hf_c = scan(xr_c, zeros_h0, 0, ctx_len, other=hb_c, gelu=gl_c)
        else:
            hf_c = scan(xr_c, zeros_h0, 0, ctx_len)
        h0_b = hb_c.reshape(batch, ctx_len, d_lru)[:, 0:1, :]
        h0_f = hf_c.reshape(batch, ctx_len, d_lru)[:, ctx_len - 1:ctx_len, :]

        xr, gl, yc = _mix_in(xs, m[3], m[4], w_in_b, *conv, layer=l, seg=GRID_W, d_lru=d_lru, conv_out=True)
        hb = scan(xr, h0_b, 1, seq)
        yr = scan(xr, h0_f, 0, seq, other=hb, gelu=gl, want_states=False)
        ln2 = (ln_g[l, 1], ln_b[l, 1])
        xs = _mix_out(xs, m[5], yr, yc, w_out_b, b_out[l], *ln2, layer=l, alpha=alpha)
        ln3 = (ln_g[l, 2], ln_b[l, 2])
        if not last:
            cs = _mix_out(cs, mc[5], yr_c, yc_c, w_out_b, b_out[l], *ln2, layer=l, alpha=alpha)
            cs = _half_ffn(cs, mc[6], mc[7], mc[8], f2_in, f2_out, *ln3, layer=l, alpha=alpha)

        xs = _half_ffn(xs, m[6], m[7], m[8], f2_in, f2_out, *ln3, layer=l, alpha=alpha)
    return xs.reshape(batch, seq, d)
```

```python
import functools

import jax
import jax.numpy as jnp
from jax import lax
from jax.experimental import pallas as pl
from jax.experimental.pallas import tpu as pltpu

GRID_W = 64
N_LRU_HEADS = 4
RG_C = 8.0
CONV_SHORT = 4
SHORT_PAD_L = 2
CONV_K = 31
N_MOD = 9
ADA_ROWS = 8
EPS = 1e-6

SUBLANES = 8
LANES = 128
CONV_ROWS = 64
CONV_LANES = LANES
PROJ_COLS = 256
VMEM_LIMIT = 56 * 1024 * 1024

F32 = jnp.float32
BF16 = jnp.bfloat16


def _params(semantics):
    return pltpu.CompilerParams(dimension_semantics=semantics, vmem_limit_bytes=VMEM_LIMIT)


def _sigmoid(x):
    return 0.5 + 0.5 * jnp.tanh(0.5 * x)


def _layernorm(z, g, b):
    mu = jnp.mean(z, axis=-1, keepdims=True)
    d = z - mu
    var = jnp.mean(d * d, axis=-1, keepdims=True)
    return d * lax.rsqrt(var + EPS) * g + b


def _gelu_tanh(x):
    return 0.5 * x * (1.0 + jnp.tanh(0.7978845608028654 * (x + 0.044715 * (x * x * x))))


def _ada_kernel(c_ref, w_ref, b_ref, o_ref):
    c = c_ref[...]
    a = c * _sigmoid(c)
    o_ref[0] = jnp.dot(a, w_ref[0], preferred_element_type=F32) + b_ref[0]


def _ada(cond, w_ada, b_ada, tn=1024):
    depth, d, n = w_ada.shape
    return pl.pallas_call(
        _ada_kernel,
        grid=(depth, n // tn),
        in_specs=[
            pl.BlockSpec((ADA_ROWS, d), lambda l, j: (0, 0)),
            pl.BlockSpec((1, d, tn), lambda l, j: (l, 0, j)),
            pl.BlockSpec((1, 1, tn), lambda l, j: (l, 0, j)),
        ],
        out_specs=pl.BlockSpec((1, ADA_ROWS, tn), lambda l, j: (l, 0, j)),
        out_shape=jax.ShapeDtypeStruct((depth, ADA_ROWS, n), F32),
        compiler_params=_params(("parallel", "parallel")),
        name="ada",
    )(cond, w_ada, b_ada.reshape(depth, 1, n))


def _ffn_kernel(s_ref, sh_ref, sc_ref, gt_ref, w1g_ref, w1u_ref, w2_ref, g_ref, b_ref,
                o_ref, h_ref, *, alpha, n_chunks):
    j = pl.program_id(1)

    @pl.when(j == 0)
    def _():
        h_ref[...] = (s_ref[...] * (1.0 + sc_ref[0]) + sh_ref[0]).astype(BF16)
        o_ref[...] = jnp.zeros_like(o_ref)

    h = h_ref[...]
    gate = jnp.dot(h, w1g_ref[...], preferred_element_type=F32)
    up = jnp.dot(h, w1u_ref[...], preferred_element_type=F32)
    act = (gate * _sigmoid(gate) * up).astype(BF16)
    o_ref[...] += jnp.dot(act, w2_ref[...], preferred_element_type=F32)

    @pl.when(j == n_chunks - 1)
    def _():
        z = alpha * s_ref[...] + 0.5 * gt_ref[0] * o_ref[...]
        o_ref[...] = _layernorm(z, g_ref[...], b_ref[...])


def _half_ffn(s, shift, scale, gate, w1, w2, g, b, *, layer, alpha, tm=512, tf=512):
    n, d = s.shape
    d_ff = w2.shape[1]
    groups = shift.shape[0]
    tiles_per_group = n // groups // tm
    n_chunks = d_ff // tf
    mod_spec = pl.BlockSpec((1, 1, d), lambda i, j: (i // tiles_per_group, 0, 0))
    vec_spec = pl.BlockSpec((1, d), lambda i, j: (0, 0))
    return pl.pallas_call(
        functools.partial(_ffn_kernel, alpha=alpha, n_chunks=n_chunks),
        grid=(n // tm, n_chunks),
        in_specs=[
            pl.BlockSpec((tm, d), lambda i, j: (i, 0)),
            mod_spec, mod_spec, mod_spec,
            pl.BlockSpec((None, d, tf), lambda i, j: (layer, 0, j)),
            pl.BlockSpec((None, d, tf), lambda i, j: (layer, 0, j + n_chunks)),
            pl.BlockSpec((None, tf, d), lambda i, j: (layer, j, 0)),
            vec_spec, vec_spec,
        ],
        out_specs=pl.BlockSpec((tm, d), lambda i, j: (i, 0)),
        out_shape=jax.ShapeDtypeStruct((n, d), F32),
        scratch_shapes=[pltpu.VMEM((tm, d), BF16)],
        compiler_params=_params(("parallel", "arbitrary")),
        name="half_ffn",
    )(s, shift, scale, gate, w1, w1, w2, g.reshape(1, d), b.reshape(1, d))


def _mix_in_kernel(x_ref, sh_ref, sc_ref, w_ref, *rest, seg, conv_out):
    if conv_out:
        c31w_ref, c31b_ref, cg_ref, cb_ref, xr_ref, gl_ref, yc_ref, upad_ref, conv_ref = rest
    else:
        (xr_ref,) = rest
    d_lru = xr_ref.shape[1]
    h = (x_ref[...] * (1.0 + sc_ref[0]) + sh_ref[0]).astype(BF16)
    if not conv_out:
        xr_ref[...] = jnp.dot(h, w_ref[...], preferred_element_type=F32)
        return
    d_conv = yc_ref.shape[1]
    tm = x_ref.shape[0]
    nseg = tm // seg
    half = CONV_K // 2
    lead = 2 * SUBLANES
    tail = upad_ref.shape[1] - lead - seg

    upad_ref[:, 0:lead, :] = jnp.zeros((nseg, lead, d_conv), F32)
    upad_ref[:, lead + seg:, :] = jnp.zeros((nseg, tail, d_conv), F32)

    def proj(c0):
        return jnp.dot(h, w_ref[:, c0:c0 + PROJ_COLS], preferred_element_type=F32)

    assert d_conv == d_lru
    for c0 in range(0, d_lru, PROJ_COLS):
        u = proj(2 * d_lru + c0) * _sigmoid(proj(2 * d_lru + d_conv + c0))
        upad_ref[:, lead:lead + seg, c0:c0 + PROJ_COLS] = u.reshape(nseg, seg, PROJ_COLS)
        xr_ref[:, c0:c0 + PROJ_COLS] = proj(c0)
        gl_ref[:, c0:c0 + PROJ_COLS] = _gelu_tanh(proj(d_lru + c0))

    offs = [lead - half + k for k in range(CONV_K)]
    slab_rows = CONV_ROWS + lead + tail
    for s in range(nseg):
        for rb in range(seg // CONV_ROWS):
            for cc in range(d_conv // CONV_LANES):
                cs = slice(cc * CONV_LANES, (cc + 1) * CONV_LANES)
                slab = upad_ref[s, rb * CONV_ROWS:rb * CONV_ROWS + slab_rows, cs]
                acc = jnp.broadcast_to(c31b_ref[:, cs], (CONV_ROWS, CONV_LANES))
                for r in range(SUBLANES):
                    rot = slab if r == 0 else pltpu.roll(slab, slab_rows - r, 0)
                    for k in range(CONV_K):
                        if offs[k] % SUBLANES == r:
                            q = offs[k] - r
                            assert q + CONV_ROWS <= slab_rows - r
                            acc = acc + c31w_ref[k:k + 1, cs] * rot[q:q + CONV_ROWS]
                row0 = s * seg + rb * CONV_ROWS
                conv_ref[row0:row0 + CONV_ROWS, cs] = acc
    y = _layernorm(conv_ref[...], cg_ref[...], cb_ref[...])
    yc_ref[...] = (y * _sigmoid(y)).astype(BF16)


def _mix_in(x, shift, scale, w_in, c31w, c31b, cg, cb, *, layer, seg, d_lru, conv_out, tm=512):
    n, d = x.shape
    d_in = w_in.shape[2]
    d_conv = (d_in - 2 * d_lru) // 2
    groups = shift.shape[0]
    tiles_per_group = n // groups // tm
    mod_spec = pl.BlockSpec((1, 1, d), lambda i: (i // tiles_per_group, 0, 0))
    row_lru = pl.BlockSpec((tm, d_lru), lambda i: (i, 0))
    in_specs = [pl.BlockSpec((tm, d), lambda i: (i, 0)), mod_spec, mod_spec]
    args = [x, shift, scale]
    if conv_out:
        in_specs += [
            pl.BlockSpec((None, d, d_in), lambda i: (layer, 0, 0)),
            pl.BlockSpec((CONV_K, d_conv), lambda i: (0, 0)),
            pl.BlockSpec((1, d_conv), lambda i: (0, 0)),
            pl.BlockSpec((1, d_conv), lambda i: (0, 0)),
            pl.BlockSpec((1, d_conv), lambda i: (0, 0)),
        ]
        args += [w_in, c31w, c31b.reshape(1, d_conv), cg.reshape(1, d_conv), cb.reshape(1, d_conv)]
        out_specs = [row_lru, row_lru, pl.BlockSpec((tm, d_conv), lambda i: (i, 0))]
        out_shape = [jax.ShapeDtypeStruct((n, d_lru), F32), jax.ShapeDtypeStruct((n, d_lru), F32),
                     jax.ShapeDtypeStruct((n, d_conv), BF16)]
        pad_rows = 2 * SUBLANES + seg + 2 * SUBLANES
        scratch = [pltpu.VMEM((tm // seg, pad_rows, d_conv), F32), pltpu.VMEM((tm, d_conv), F32)]
    else:
        in_specs += [pl.BlockSpec((None, d, d_lru), lambda i: (layer, 0, 0))]
        args += [w_in]
        out_specs = [row_lru]
        out_shape = [jax.ShapeDtypeStruct((n, d_lru), F32)]
        scratch = []
    return pl.pallas_call(
        functools.partial(_mix_in_kernel, seg=seg, conv_out=conv_out),
        grid=(n // tm,),
        in_specs=in_specs,
        out_specs=out_specs,
        out_shape=out_shape,
        scratch_shapes=scratch,
        compiler_params=_params(("parallel",)),
        name="mix_in",
    )(*args)


def _scan_kernel(xr_ref, xp_ref, xn_ref, c4w_ref, c4b_ref, wr_ref, br_ref, wi_ref, bi_ref,
                 lam_ref, h0_ref, *rest, reverse, tiles_per_seq, n_tiles, fuse_out):
    if fuse_out:
        hb_ref, gl_ref, o_ref, hs_ref, ext_ref, a_ref, b_ref, carry_ref = rest
    else:
        hs_ref, ext_ref, a_ref, b_ref, carry_ref = rest
    i = pl.program_id(0)
    t = (n_tiles - 1 - i) if reverse else i
    p = t % tiles_per_seq
    first = p == 0
    last = p == tiles_per_seq - 1
    tm, c = xr_ref.shape
    hd = c // N_LRU_HEADS

    ext_ref[0:SUBLANES, :] = jnp.where(first, 0.0, xp_ref[...])
    ext_ref[SUBLANES:SUBLANES + tm, :] = xr_ref[...]
    ext_ref[SUBLANES + tm:, :] = jnp.where(last, 0.0, xn_ref[...])
    xc = jnp.zeros((tm, c), F32) + c4b_ref[...]
    for k in range(CONV_SHORT):
        off = SUBLANES - SHORT_PAD_L + k
        xc = xc + c4w_ref[k:k + 1, :] * ext_ref[off:off + tm, :]

    xcb = xc.astype(BF16)
    lam = lam_ref[...]
    softplus_neg_lam = jnp.maximum(-lam, 0.0) + jnp.log(1.0 + jnp.exp(-jnp.abs(lam)))
    coef = -RG_C * softplus_neg_lam
    for hh in range(N_LRU_HEADS):
        cs = slice(hh * hd, (hh + 1) * hd)
        xh = xcb[:, cs]
        r = _sigmoid(jnp.dot(xh, wr_ref[hh], preferred_element_type=F32) + br_ref[:, cs])
        ig = _sigmoid(jnp.dot(xh, wi_ref[hh], preferred_element_type=F32) + bi_ref[:, cs])
        log_a = coef[:, cs] * r
        a = jnp.exp(log_a)
        a_ref[:, cs] = a
        b_ref[:, cs] = jnp.sqrt(1.0 - jnp.exp(2.0 * log_a)) * (ig * xc[:, cs])

    start = last if reverse else first

    @pl.when(start)
    def _():
        carry_ref[...] = h0_ref[0]

    n_groups = tm // SUBLANES
    row = lax.broadcasted_iota(jnp.int32, (SUBLANES, c), 0)

    def group(k, carry):
        kk = (n_groups - 1 - k) if reverse else k
        r0 = pl.multiple_of(kk * SUBLANES, SUBLANES)
        a = a_ref[pl.ds(r0, SUBLANES), :]
        b = b_ref[pl.ds(r0, SUBLANES), :]
        for s in (1, 2, 4):
            if reverse:
                shift, keep = SUBLANES - s, row < SUBLANES - s
            else:
                shift, keep = s, row >= s
            a_prev = jnp.where(keep, pltpu.roll(a, shift, 0), 1.0)
            b_prev = jnp.where(keep, pltpu.roll(b, shift, 0), 0.0)
            b = b + a * b_prev
            a = a * a_prev
        h = b + a * carry
        hs_ref[pl.ds(r0, SUBLANES), :] = h
        return h[0:1, :] if reverse else h[SUBLANES - 1:SUBLANES, :]

    carry_ref[...] = lax.fori_loop(0, n_groups, group, carry_ref[...])

    if fuse_out:
        o_ref[...] = ((hs_ref[...] + hb_ref[...]) * gl_ref[...]).astype(BF16)


def _scan(xr, c4w, c4b, wr, br, wi, bi, lam, h0, *, layer, direction, seq_len, other=None, gelu=None,
          want_states=True, tm=512):
    reverse = direction == 1
    n, c = xr.shape
    tm = min(tm, seq_len)
    n_tiles = n // tm
    tiles_per_seq = seq_len // tm
    hd = c // N_LRU_HEADS
    fuse_out = other is not None
    blocks_per_tile = tm // SUBLANES
    n_blocks = n // SUBLANES

    def tile(i):
        return (n_tiles - 1 - i) if reverse else i

    row_spec = pl.BlockSpec((tm, c), lambda i: (tile(i), 0))
    vec_spec = pl.BlockSpec((1, c), lambda i: (0, 0))
    w_spec = pl.BlockSpec((None, None, N_LRU_HEADS, hd, hd), lambda i: (layer, direction, 0, 0, 0))
    in_specs = [
        row_spec,
        pl.BlockSpec((SUBLANES, c), lambda i: (jnp.maximum(tile(i) * blocks_per_tile - 1, 0), 0)),
        pl.BlockSpec((SUBLANES, c), lambda i: (jnp.minimum((tile(i) + 1) * blocks_per_tile, n_blocks - 1), 0)),
        pl.BlockSpec((CONV_SHORT, c), lambda i: (0, 0)),
        vec_spec, w_spec, vec_spec, w_spec, vec_spec, vec_spec,
        pl.BlockSpec((1, 1, c), lambda i: (tile(i) // tiles_per_seq, 0, 0)),
    ]
    args = [xr, xr, xr, c4w, c4b.reshape(1, c), wr, br.reshape(1, c), wi, bi.reshape(1, c),
            lam.reshape(1, c), h0]
    scratch = [pltpu.VMEM((tm + 2 * SUBLANES, c), F32), pltpu.VMEM((tm, c), F32),
               pltpu.VMEM((tm, c), F32), pltpu.VMEM((1, c), F32)]
    states = jax.ShapeDtypeStruct((n, c), F32)
    fused = jax.ShapeDtypeStruct((n, c), BF16)
    if fuse_out:
        in_specs += [row_spec, row_spec]
        args += [other, gelu]
    if fuse_out and want_states:
        out_specs, out_shape = [row_spec, row_spec], [fused, states]
    elif fuse_out:
        out_specs, out_shape = row_spec, fused
        scratch = [pltpu.VMEM((tm, c), F32)] + scratch
    else:
        out_specs, out_shape = row_spec, states
    return pl.pallas_call(
        functools.partial(_scan_kernel, reverse=reverse, tiles_per_seq=tiles_per_seq,
                          n_tiles=n_tiles, fuse_out=fuse_out),
        grid=(n_tiles,),
        in_specs=in_specs,
        out_specs=out_specs,
        out_shape=out_shape,
        scratch_shapes=scratch,
        compiler_params=_params(("arbitrary",)),
        name="lru_scan",
    )(*args)


def _mix_out_kernel(x_ref, gt_ref, yr_ref, yc_ref, wa_ref, wb_ref, bo_ref, g_ref, b_ref, o_ref, *, alpha):
    y = jnp.dot(yr_ref[...], wa_ref[...], preferred_element_type=F32)
    y = y + jnp.dot(yc_ref[...], wb_ref[...], preferred_element_type=F32) + bo_ref[...]
    z = alpha * x_ref[...] + gt_ref[0] * y
    o_ref[...] = _layernorm(z, g_ref[...], b_ref[...])


def _mix_out(x, gate, y_rec, y_conv, w_out, b_out, g, b, *, layer, alpha, tm=512):
    n, d = x.shape
    d_lru = y_rec.shape[1]
    d_conv = y_conv.shape[1]
    groups = gate.shape[0]
    tiles_per_group = n // groups // tm
    vec_spec = pl.BlockSpec((1, d), lambda i: (0, 0))
    assert d_lru == d_conv
    return pl.pallas_call(
        functools.partial(_mix_out_kernel, alpha=alpha),
        grid=(n // tm,),
        in_specs=[
            pl.BlockSpec((tm, d), lambda i: (i, 0)),
            pl.BlockSpec((1, 1, d), lambda i: (i // tiles_per_group, 0, 0)),
            pl.BlockSpec((tm, d_lru), lambda i: (i, 0)),
            pl.BlockSpec((tm, d_conv), lambda i: (i, 0)),
            pl.BlockSpec((None, d_lru, d), lambda i: (layer, 0, 0)),
            pl.BlockSpec((None, d_conv, d), lambda i: (layer, 1, 0)),
            vec_spec, vec_spec, vec_spec,
        ],
        out_specs=pl.BlockSpec((tm, d), lambda i: (i, 0)),
        out_shape=jax.ShapeDtypeStruct((n, d), F32),
        compiler_params=_params(("parallel",)),
        name="mix_out",
    )(x, gate, y_rec, y_conv, w_out, w_out, b_out.reshape(1, d), g.reshape(1, d), b.reshape(1, d))


def kernel(x, c, ctx, c_ctx, w_ada, b_ada, ln_g, ln_b, ff1_in, ff1_out, ff2_in, ff2_out, w_in, conv4_w,
           conv4_b, w_rg, b_rg, w_ig, b_ig, lam, conv31_w, conv31_b, cln_g, cln_b, w_out, b_out):
    batch, seq, d = x.shape
    ctx_len = ctx.shape[1]
    depth = w_ada.shape[0]
    d_lru = conv4_w.shape[-1]
    alpha = (2 * depth) ** 0.25

    cond = jnp.concatenate([c, c_ctx[None, :], jnp.zeros((ADA_ROWS - batch - 1, d), F32)], axis=0)
    mods = _ada(cond, w_ada, b_ada).reshape(depth, ADA_ROWS, N_MOD, d)

    f1_in, f1_out = ff1_in.astype(BF16), ff1_out.astype(BF16)
    f2_in, f2_out = ff2_in.astype(BF16), ff2_out.astype(BF16)
    w_in_b, w_out_b = w_in.astype(BF16), w_out.astype(BF16)
    wr, wi = w_rg.astype(BF16), w_ig.astype(BF16)

    xs = x.reshape(batch * seq, d)
    cs = ctx.reshape(batch * ctx_len, d)
    zeros_h0 = jnp.zeros((batch, 1, d_lru), F32)
    for l in range(depth):
        last = l == depth - 1
        m = [mods[l, 0:batch, k][:, None, :] for k in range(N_MOD)]
        mc = [mods[l, batch:batch + 1, k][:, None, :] for k in range(N_MOD)]
        conv = (conv31_w[l], conv31_b[l], cln_g[l], cln_b[l])

        def scan(xr_, h0_, direction, seq_len, **kw):
            return _scan(xr_, conv4_w[l], conv4_b[l], wr, b_rg[l, direction], wi, b_ig[l, direction],
                         lam[l, direction], h0_, layer=l, direction=direction, seq_len=seq_len, **kw)

        ln1 = (ln_g[l, 0], ln_b[l, 0])
        xs = _half_ffn(xs, m[0], m[1], m[2], f1_in, f1_out, *ln1, layer=l, alpha=alpha)
        cs = _half_ffn(cs, mc[0], mc[1], mc[2], f1_in, f1_out, *ln1, layer=l, alpha=alpha)

        if not last:
            xr_c, gl_c, yc_c = _mix_in(cs, mc[3], mc[4], w_in_b, *conv, layer=l, seg=ctx_len, d_lru=d_lru,
                                       conv_out=True)
        else:
            (xr_c,) = _mix_in(cs, mc[3], mc[4], w_in_b, *conv, layer=l, seg=ctx_len, d_lru=d_lru,
                              conv_out=False)
        hb_c = scan(xr_c, zeros_h0, 1, ctx_len)
        if not last:
            yr_c, hf_c = scan(xr_c, zeros_h0, 0, ctx_len, other=hb_c, gelu=gl_c)
        else:
            hf_c = scan(xr_c, zeros_h0, 0, ctx_len)
        h0_b = hb_c.reshape(batch, ctx_len, d_lru)[:, 0:1, :]
        h0_f = hf_c.reshape(batch, ctx_len, d_lru)[:, ctx_len - 1:ctx_len, :]

        xr, gl, yc = _mix_in(xs, m[3], m[4], w_in_b, *conv, layer=l, seg=GRID_W, d_lru=d_lru, conv_out=True)
        hb = scan(xr, h0_b, 1, seq)
        yr = scan(xr, h0_f, 0, seq, other=hb, gelu=gl, want_states=False)
        ln2 = (ln_g[l, 1], ln_b[l, 1])
        xs = _mix_out(xs, m[5], yr, yc, w_out_b, b_out[l], *ln2, layer=l, alpha=alpha)
        ln3 = (ln_g[l, 2], ln_b[l, 2])
        if not last:
            cs = _mix_out(cs, mc[5], yr_c, yc_c, w_out_b, b_out[l], *ln2, layer=l, alpha=alpha)
            cs = _half_ffn(cs, mc[6], mc[7], mc[8], f2_in, f2_out, *ln3, layer=l, alpha=alpha)

        xs = _half_ffn(xs, m[6], m[7], m[8], f2_in, f2_out, *ln3, layer=l, alpha=alpha)
    return xs.reshape(batch, seq, d)
```

```python
import functools

import jax
import jax.numpy as jnp
from jax import lax
from jax.experimental import pallas as pl
from jax.experimental.pallas import tpu as pltpu

GRID_W = 64
N_LRU_HEADS = 4
RG_C = 8.0
CONV_SHORT = 4
SHORT_PAD_L = 2
CONV_K = 31
N_MOD = 9
ADA_ROWS = 8
EPS = 1e-6

SUBLANES = 8
LANES = 128
CONV_ROWS = 64
CONV_LANES = LANES
FFN_TAIL_ROWS = 128
CAST_BLOCK = LANES
PROJ_COLS = 256
VMEM_LIMIT = 56 * 1024 * 1024

F32 = jnp.float32
BF16 = jnp.bfloat16


def _params(semantics):
    return pltpu.CompilerParams(dimension_semantics=semantics, vmem_limit_bytes=VMEM_LIMIT)


def _sigmoid(x):
    return 0.5 + 0.5 * jnp.tanh(0.5 * x)


def _layernorm(z, g, b):
    mu = jnp.mean(z, axis=-1, keepdims=True)
    d = z - mu
    var = jnp.mean(d * d, axis=-1, keepdims=True)
    return d * lax.rsqrt(var + EPS) * g + b


def _gelu_tanh(x):
    return 0.5 * x * (1.0 + jnp.tanh(0.7978845608028654 * (x + 0.044715 * (x * x * x))))


def _ada_kernel(c_ref, w_ref, b_ref, o_ref):
    c = c_ref[...]
    a = c * _sigmoid(c)
    o_ref[0] = jnp.dot(a, w_ref[0], preferred_element_type=F32) + b_ref[0]


def _ada(cond, w_ada, b_ada, tn=1024):
    depth, d, n = w_ada.shape
    return pl.pallas_call(
        _ada_kernel,
        grid=(depth, n // tn),
        in_specs=[
            pl.BlockSpec((ADA_ROWS, d), lambda l, j: (0, 0)),
            pl.BlockSpec((1, d, tn), lambda l, j: (l, 0, j)),
            pl.BlockSpec((1, 1, tn), lambda l, j: (l, 0, j)),
        ],
        out_specs=pl.BlockSpec((1, ADA_ROWS, tn), lambda l, j: (l, 0, j)),
        out_shape=jax.ShapeDtypeStruct((depth, ADA_ROWS, n), F32),
        compiler_params=_params(("parallel", "parallel")),
        name="ada",
    )(cond, w_ada, b_ada.reshape(depth, 1, n))


def _ffn_kernel(s_ref, sh_ref, sc_ref, gt_ref, w1g_ref, w1u_ref, w2_ref, g_ref, b_ref, *rest,
                alpha, n_chunks, n_casts):
    cast_in, o_ref, cast_out, h_ref = rest[:n_casts], rest[n_casts], rest[n_casts + 1:-1], rest[-1]
    j = pl.program_id(1)
    tm = s_ref.shape[0]
    assert n_chunks >= 2

    def chunk(first, last):
        for src_ref, dst_ref in zip(cast_in, cast_out):
            dst_ref[...] = src_ref[...].astype(BF16)
        if first:
            h = (s_ref[...] * (1.0 + sc_ref[0]) + sh_ref[0]).astype(BF16)
            h_ref[...] = h
        else:
            h = h_ref[...]
        gate = jnp.dot(h, w1g_ref[...], preferred_element_type=F32)
        up = jnp.dot(h, w1u_ref[...], preferred_element_type=F32)
        act = (gate * _sigmoid(gate) * up).astype(BF16)
        if first:
            o_ref[...] = jnp.dot(act, w2_ref[...], preferred_element_type=F32)
        elif not last:
            o_ref[...] += jnp.dot(act, w2_ref[...], preferred_element_type=F32)
        else:
            for r0 in range(0, tm, FFN_TAIL_ROWS):
                rows = slice(r0, r0 + FFN_TAIL_ROWS)
                y = o_ref[rows, :] + jnp.dot(act[rows], w2_ref[...], preferred_element_type=F32)
                z = alpha * s_ref[rows, :] + 0.5 * gt_ref[0] * y
                o_ref[rows, :] = _layernorm(z, g_ref[...], b_ref[...])

    pl.when(j == 0)(functools.partial(chunk, True, False))
    pl.when((j > 0) & (j < n_chunks - 1))(functools.partial(chunk, False, False))
    pl.when(j == n_chunks - 1)(functools.partial(chunk, False, True))


def _cast_specs(w, axis, n_chunks, n_steps):
    n_layers, rows, cols = w.shape
    per_layer = w.shape[axis] // CAST_BLOCK
    n_blocks = n_layers * per_layer
    assert n_blocks <= n_steps, "not enough grid steps to cast this weight"

    def index(i, j):
        q = jnp.minimum(i * n_chunks + j, n_blocks - 1)
        blk = q % per_layer
        return (q // per_layer, blk, 0) if axis == 1 else (q // per_layer, 0, blk)

    shape = (None, CAST_BLOCK, cols) if axis == 1 else (None, rows, CAST_BLOCK)
    return pl.BlockSpec(shape, index), pl.BlockSpec(shape, index), jax.ShapeDtypeStruct(w.shape, BF16)


def _half_ffn(s, shift, scale, gate, w1, w2, g, b, *, layer, alpha, casts=(), tm=512, tf=512):
    n, d = s.shape
    d_ff = w2.shape[1]
    groups = shift.shape[0]
    tiles_per_group = n // groups // tm
    n_chunks = d_ff // tf
    grid = (n // tm, n_chunks)
    mod_spec = pl.BlockSpec((1, 1, d), lambda i, j: (i // tiles_per_group, 0, 0))
    vec_spec = pl.BlockSpec((1, d), lambda i, j: (0, 0))
    cast_specs = [_cast_specs(w, axis, n_chunks, grid[0] * grid[1]) for w, axis in casts]
    out = pl.pallas_call(
        functools.partial(_ffn_kernel, alpha=alpha, n_chunks=n_chunks, n_casts=len(casts)),
        grid=grid,
        in_specs=[
            pl.BlockSpec((tm, d), lambda i, j: (i, 0)),
            mod_spec, mod_spec, mod_spec,
            pl.BlockSpec((None, d, tf), lambda i, j: (layer, 0, j)),
            pl.BlockSpec((None, d, tf), lambda i, j: (layer, 0, j + n_chunks)),
            pl.BlockSpec((None, tf, d), lambda i, j: (layer, j, 0)),
            vec_spec, vec_spec,
        ] + [c[0] for c in cast_specs],
        out_specs=[pl.BlockSpec((tm, d), lambda i, j: (i, 0))] + [c[1] for c in cast_specs],
        out_shape=[jax.ShapeDtypeStruct((n, d), F32)] + [c[2] for c in cast_specs],
        scratch_shapes=[pltpu.VMEM((tm, d), BF16)],
        compiler_params=_params(("arbitrary" if casts else "parallel", "arbitrary")),
        name="half_ffn",
    )(s, shift, scale, gate, w1, w1, w2, g.reshape(1, d), b.reshape(1, d), *[w for w, _ in casts])
    return out if casts else out[0]


def _mix_in_kernel(x_ref, sh_ref, sc_ref, w_ref, *rest, seg, conv_out):
    if conv_out:
        c31w_ref, c31b_ref, cg_ref, cb_ref, xr_ref, gl_ref, yc_ref, upad_ref, conv_ref = rest
    else:
        (xr_ref,) = rest
    d_lru = xr_ref.shape[1]
    h = (x_ref[...] * (1.0 + sc_ref[0]) + sh_ref[0]).astype(BF16)
    if not conv_out:
        xr_ref[...] = jnp.dot(h, w_ref[...], preferred_element_type=F32)
        return
    d_conv = yc_ref.shape[1]
    tm = x_ref.shape[0]
    nseg = tm // seg
    half = CONV_K // 2
    lead = 2 * SUBLANES
    tail = upad_ref.shape[1] - lead - seg

    upad_ref[:, 0:lead, :] = jnp.zeros((nseg, lead, d_conv), F32)
    upad_ref[:, lead + seg:, :] = jnp.zeros((nseg, tail, d_conv), F32)

    def proj(c0):
        return jnp.dot(h, w_ref[:, c0:c0 + PROJ_COLS], preferred_element_type=F32)

    assert d_conv == d_lru
    for c0 in range(0, d_lru, PROJ_COLS):
        u = proj(2 * d_lru + c0) * _sigmoid(proj(2 * d_lru + d_conv + c0))
        upad_ref[:, lead:lead + seg, c0:c0 + PROJ_COLS] = u.reshape(nseg, seg, PROJ_COLS)
        xr_ref[:, c0:c0 + PROJ_COLS] = proj(c0)
        gl_ref[:, c0:c0 + PROJ_COLS] = _gelu_tanh(proj(d_lru + c0))

    offs = [lead - half + k for k in range(CONV_K)]
    slab_rows = CONV_ROWS + lead + tail
    for s in range(nseg):
        for rb in range(seg // CONV_ROWS):
            for cc in range(d_conv // CONV_LANES):
                cs = slice(cc * CONV_LANES, (cc + 1) * CONV_LANES)
                slab = upad_ref[s, rb * CONV_ROWS:rb * CONV_ROWS + slab_rows, cs]
                acc = jnp.broadcast_to(c31b_ref[:, cs], (CONV_ROWS, CONV_LANES))
                for r in range(SUBLANES):
                    rot = slab if r == 0 else pltpu.roll(slab, slab_rows - r, 0)
                    for k in range(CONV_K):
                        if offs[k] % SUBLANES == r:
                            q = offs[k] - r
                            assert q + CONV_ROWS <= slab_rows - r
                            acc = acc + c31w_ref[k:k + 1, cs] * rot[q:q + CONV_ROWS]
                row0 = s * seg + rb * CONV_ROWS
                conv_ref[row0:row0 + CONV_ROWS, cs] = acc
    y = _layernorm(conv_ref[...], cg_ref[...], cb_ref[...])
    yc_ref[...] = (y * _sigmoid(y)).astype(BF16)


def _mix_in(x, shift, scale, w_in, c31w, c31b, cg, cb, *, layer, seg, d_lru, conv_out, tm=512):
    n, d = x.shape
    d_in = w_in.shape[2]
    d_conv = (d_in - 2 * d_lru) // 2
    groups = shift.shape[0]
    tiles_per_group = n // groups // tm
    mod_spec = pl.BlockSpec((1, 1, d), lambda i: (i // tiles_per_group, 0, 0))
    row_lru = pl.BlockSpec((tm, d_lru), lambda i: (i, 0))
    in_specs = [pl.BlockSpec((tm, d), lambda i: (i, 0)), mod_spec, mod_spec]
    args = [x, shift, scale]
    if conv_out:
        in_specs += [
            pl.BlockSpec((None, d, d_in), lambda i: (layer, 0, 0)),
            pl.BlockSpec((CONV_K, d_conv), lambda i: (0, 0)),
            pl.BlockSpec((1, d_conv), lambda i: (0, 0)),
            pl.BlockSpec((1, d_conv), lambda i: (0, 0)),
            pl.BlockSpec((1, d_conv), lambda i: (0, 0)),
        ]
        args += [w_in, c31w, c31b.reshape(1, d_conv), cg.reshape(1, d_conv), cb.reshape(1, d_conv)]
        out_specs = [row_lru, row_lru, pl.BlockSpec((tm, d_conv), lambda i: (i, 0))]
        out_shape = [jax.ShapeDtypeStruct((n, d_lru), F32), jax.ShapeDtypeStruct((n, d_lru), F32),
                     jax.ShapeDtypeStruct((n, d_conv), BF16)]
        pad_rows = 2 * SUBLANES + seg + 2 * SUBLANES
        scratch = [pltpu.VMEM((tm // seg, pad_rows, d_conv), F32), pltpu.VMEM((tm, d_conv), F32)]
    else:
        in_specs += [pl.BlockSpec((None, d, d_lru), lambda i: (layer, 0, 0))]
        args += [w_in]
        out_specs = [row_lru]
        out_shape = [jax.ShapeDtypeStruct((n, d_lru), F32)]
        scratch = []
    return pl.pallas_call(
        functools.partial(_mix_in_kernel, seg=seg, conv_out=conv_out),
        grid=(n // tm,),
        in_specs=in_specs,
        out_specs=out_specs,
        out_shape=out_shape,
        scratch_shapes=scratch,
        compiler_params=_params(("parallel",)),
        name="mix_in",
    )(*args)


def _scan_kernel(xr_ref, xp_ref, xn_ref, c4w_ref, c4b_ref, wr_ref, br_ref, wi_ref, bi_ref,
                 lam_ref, h0_ref, *rest, reverse, tiles_per_seq, n_tiles, fuse_out):
    if fuse_out:
        hb_ref, gl_ref, o_ref, hs_ref, ext_ref, a_ref, b_ref, carry_ref = rest
    else:
        hs_ref, ext_ref, a_ref, b_ref, carry_ref = rest
    i = pl.program_id(0)
    t = (n_tiles - 1 - i) if reverse else i
    p = t % tiles_per_seq
    first = p == 0
    last = p == tiles_per_seq - 1
    tm, c = xr_ref.shape
    hd = c // N_LRU_HEADS

    ext_ref[0:SUBLANES, :] = jnp.where(first, 0.0, xp_ref[...])
    ext_ref[SUBLANES:SUBLANES + tm, :] = xr_ref[...]
    ext_ref[SUBLANES + tm:, :] = jnp.where(last, 0.0, xn_ref[...])
    xc = jnp.zeros((tm, c), F32) + c4b_ref[...]
    for k in range(CONV_SHORT):
        off = SUBLANES - SHORT_PAD_L + k
        xc = xc + c4w_ref[k:k + 1, :] * ext_ref[off:off + tm, :]

    xcb = xc.astype(BF16)
    lam = lam_ref[...]
    softplus_neg_lam = jnp.maximum(-lam, 0.0) + jnp.log(1.0 + jnp.exp(-jnp.abs(lam)))
    coef = -RG_C * softplus_neg_lam
    for hh in range(N_LRU_HEADS):
        cs = slice(hh * hd, (hh + 1) * hd)
        xh = xcb[:, cs]
        r = _sigmoid(jnp.dot(xh, wr_ref[hh], preferred_element_type=F32) + br_ref[:, cs])
        ig = _sigmoid(jnp.dot(xh, wi_ref[hh], preferred_element_type=F32) + bi_ref[:, cs])
        log_a = coef[:, cs] * r
        a = jnp.exp(log_a)
        a_ref[:, cs] = a
        b_ref[:, cs] = jnp.sqrt(1.0 - jnp.exp(2.0 * log_a)) * (ig * xc[:, cs])

    start = last if reverse else first

    @pl.when(start)
    def _():
        carry_ref[...] = h0_ref[0]

    n_groups = tm // SUBLANES
    row = lax.broadcasted_iota(jnp.int32, (SUBLANES, c), 0)

    def group(k, carry):
        kk = (n_groups - 1 - k) if reverse else k
        r0 = pl.multiple_of(kk * SUBLANES, SUBLANES)
        a = a_ref[pl.ds(r0, SUBLANES), :]
        b = b_ref[pl.ds(r0, SUBLANES), :]
        for s in (1, 2, 4):
            if reverse:
                shift, keep = SUBLANES - s, row < SUBLANES - s
            else:
                shift, keep = s, row >= s
            a_prev = jnp.where(keep, pltpu.roll(a, shift, 0), 1.0)
            b_prev = jnp.where(keep, pltpu.roll(b, shift, 0), 0.0)
            b = b + a * b_prev
            a = a * a_prev
        h = b + a * carry
        hs_ref[pl.ds(r0, SUBLANES), :] = h
        return h[0:1, :] if reverse else h[SUBLANES - 1:SUBLANES, :]

    carry_ref[...] = lax.fori_loop(0, n_groups, group, carry_ref[...])

    if fuse_out:
        o_ref[...] = ((hs_ref[...] + hb_ref[...]) * gl_ref[...]).astype(BF16)


def _scan(xr, c4w, c4b, wr, br, wi, bi, lam, h0, *, layer, direction, seq_len, other=None, gelu=None,
          want_states=True, tm=512):
    reverse = direction == 1
    n, c = xr.shape
    tm = min(tm, seq_len)
    n_tiles = n // tm
    tiles_per_seq = seq_len // tm
    hd = c // N_LRU_HEADS
    fuse_out = other is not None
    blocks_per_tile = tm // SUBLANES
    n_blocks = n // SUBLANES

    def tile(i):
        return (n_tiles - 1 - i) if reverse else i

    row_spec = pl.BlockSpec((tm, c), lambda i: (tile(i), 0))
    vec_spec = pl.BlockSpec((1, c), lambda i: (0, 0))
    w_spec = pl.BlockSpec((None, None, N_LRU_HEADS, hd, hd), lambda i: (layer, direction, 0, 0, 0))
    in_specs = [
        row_spec,
        pl.BlockSpec((SUBLANES, c), lambda i: (jnp.maximum(tile(i) * blocks_per_tile - 1, 0), 0)),
        pl.BlockSpec((SUBLANES, c), lambda i: (jnp.minimum((tile(i) + 1) * blocks_per_tile, n_blocks - 1), 0)),
        pl.BlockSpec((CONV_SHORT, c), lambda i: (0, 0)),
        vec_spec, w_spec, vec_spec, w_spec, vec_spec, vec_spec,
        pl.BlockSpec((1, 1, c), lambda i: (tile(i) // tiles_per_seq, 0, 0)),
    ]
    args = [xr, xr, xr, c4w, c4b.reshape(1, c), wr, br.reshape(1, c), wi, bi.reshape(1, c),
            lam.reshape(1, c), h0]
    scratch = [pltpu.VMEM((tm + 2 * SUBLANES, c), F32), pltpu.VMEM((tm, c), F32),
               pltpu.VMEM((tm, c), F32), pltpu.VMEM((1, c), F32)]
    states = jax.ShapeDtypeStruct((n, c), F32)
    fused = jax.ShapeDtypeStruct((n, c), BF16)
    if fuse_out:
        in_specs += [row_spec, row_spec]
        args += [other, gelu]
    if fuse_out and want_states:
        out_specs, out_shape = [row_spec, row_spec], [fused, states]
    elif fuse_out:
        out_specs, out_shape = row_spec, fused
        scratch = [pltpu.VMEM((tm, c), F32)] + scratch
    else:
        out_specs, out_shape = row_spec, states
    return pl.pallas_call(
        functools.partial(_scan_kernel, reverse=reverse, tiles_per_seq=tiles_per_seq,
                          n_tiles=n_tiles, fuse_out=fuse_out),
        grid=(n_tiles,),
        in_specs=in_specs,
        out_specs=out_specs,
        out_shape=out_shape,
        scratch_shapes=scratch,
        compiler_params=_params(("arbitrary",)),
        name="lru_scan",
    )(*args)


def _mix_out_kernel(x_ref, gt_ref, yr_ref, yc_ref, wa_ref, wb_ref, bo_ref, g_ref, b_ref, o_ref, *, alpha):
    y = jnp.dot(yr_ref[...], wa_ref[...], preferred_element_type=F32)
    y = y + jnp.dot(yc_ref[...], wb_ref[...], preferred_element_type=F32) + bo_ref[...]
    z = alpha * x_ref[...] + gt_ref[0] * y
    o_ref[...] = _layernorm(z, g_ref[...], b_ref[...])


def _mix_out(x, gate, y_rec, y_conv, w_out, b_out, g, b, *, layer, alpha, tm=512):
    n, d = x.shape
    d_lru = y_rec.shape[1]
    d_conv = y_conv.shape[1]
    groups = gate.shape[0]
    tiles_per_group = n // groups // tm
    vec_spec = pl.BlockSpec((1, d), lambda i: (0, 0))
    assert d_lru == d_conv
    return pl.pallas_call(
        functools.partial(_mix_out_kernel, alpha=alpha),
        grid=(n // tm,),
        in_specs=[
            pl.BlockSpec((tm, d), lambda i: (i, 0)),
            pl.BlockSpec((1, 1, d), lambda i: (i // tiles_per_group, 0, 0)),
            pl.BlockSpec((tm, d_lru), lambda i: (i, 0)),
            pl.BlockSpec((tm, d_conv), lambda i: (i, 0)),
            pl.BlockSpec((None, d_lru, d), lambda i: (layer, 0, 0)),
            pl.BlockSpec((None, d_conv, d), lambda i: (layer, 1, 0)),
            vec_spec, vec_spec, vec_spec,
        ],
        out_specs=pl.BlockSpec((tm, d), lambda i: (i, 0)),
        out_shape=jax.ShapeDtypeStruct((n, d), F32),
        compiler_params=_params(("parallel",)),
        name="mix_out",
    )(x, gate, y_rec, y_conv, w_out, w_out, b_out.reshape(1, d), g.reshape(1, d), b.reshape(1, d))


def kernel(x, c, ctx, c_ctx, w_ada, b_ada, ln_g, ln_b, ff1_in, ff1_out, ff2_in, ff2_out, w_in, conv4_w,
           conv4_b, w_rg, b_rg, w_ig, b_ig, lam, conv31_w, conv31_b, cln_g, cln_b, w_out, b_out):
    batch, seq, d = x.shape
    ctx_len = ctx.shape[1]
    depth = w_ada.shape[0]
    d_lru = conv4_w.shape[-1]
    alpha = (2 * depth) ** 0.25

    cond = jnp.concatenate([c, c_ctx[None, :], jnp.zeros((ADA_ROWS - batch - 1, d), F32)], axis=0)
    mods = _ada(cond, w_ada, b_ada).reshape(depth, ADA_ROWS, N_MOD, d)

    f1_in, f1_out = ff1_in.astype(BF16), ff1_out.astype(BF16)
    w_out_b = w_out.astype(BF16)
    wr, wi = w_rg.astype(BF16), w_ig.astype(BF16)
    f2_in = f2_out = w_in_b = None

    xs = x.reshape(batch * seq, d)
    cs = ctx.reshape(batch * ctx_len, d)
    zeros_h0 = jnp.zeros((batch, 1, d_lru), F32)
    for l in range(depth):
        last = l == depth - 1
        m = [mods[l, 0:batch, k][:, None, :] for k in range(N_MOD)]
        mc = [mods[l, batch:batch + 1, k][:, None, :] for k in range(N_MOD)]
        conv = (conv31_w[l], conv31_b[l], cln_g[l], cln_b[l])

        def scan(xr_, h0_, direction, seq_len, **kw):
            return _scan(xr_, conv4_w[l], conv4_b[l], wr, b_rg[l, direction], wi, b_ig[l, direction],
                         lam[l, direction], h0_, layer=l, direction=direction, seq_len=seq_len, **kw)

        ln1 = (ln_g[l, 0], ln_b[l, 0])
        if l == 0:
            xs, f2_in, f2_out, w_in_b = _half_ffn(xs, m[0], m[1], m[2], f1_in, f1_out, *ln1, layer=l, alpha=alpha,
                                                  casts=((ff2_in, 2), (ff2_out, 1), (w_in, 2)))
        else:
            xs = _half_ffn(xs, m[0], m[1], m[2], f1_in, f1_out, *ln1, layer=l, alpha=alpha)
        cs = _half_ffn(cs, mc[0], mc[1], mc[2], f1_in, f1_out, *ln1, layer=l, alpha=alpha)

        if not last:
            xr_c, gl_c, yc_c = _mix_in(cs, mc[3], mc[4], w_in_b, *conv, layer=l, seg=ctx_len, d_lru=d_lru,
                                       conv_out=True)
        else:
            (xr_c,) = _mix_in(cs, mc[3], mc[4], w_in_b, *conv, layer=l, seg=ctx_len, d_lru=d_lru,
                              conv_out=False)
        hb_c = scan(xr_c, zeros_h0, 1, ctx_len)
        if not last:
            yr_c, hf_c = scan(xr_c, zeros_h0, 0, ctx_len, other=hb_c, gelu=gl_c)
        else:
            hf_c = scan(xr_c, zeros_h0, 0, ctx_len)
        h0_b = hb_c.reshape(batch, ctx_len, d_lru)[:, 0:1, :]
        h0_f = hf_c.reshape(batch, ctx_len, d_lru)[:, ctx_len - 1:ctx_len, :]

        xr, gl, yc = _mix_in(xs, m[3], m[4], w_in_b, *conv, layer=l, seg=GRID_W, d_lru=d_lru, conv_out=True)
        hb = scan(xr, h0_b, 1, seq)
        yr = scan(xr, h0_f, 0, seq, other=hb, gelu=gl, want_states=False)
        ln2 = (ln_g[l, 1], ln_b[l, 1])
        xs = _mix_out(xs, m[5], yr, yc, w_out_b, b_out[l], *ln2, layer=l, alpha=alpha)
        ln3 = (ln_g[l, 2], ln_b[l, 2])
        if not last:
            cs = _mix_out(cs, mc[5], yr_c, yc_c, w_out_b, b_out[l], *ln2, layer=l, alpha=alpha)
            cs = _half_ffn(cs, mc[6], mc[7], mc[8], f2_in, f2_out, *ln3, layer=l, alpha=alpha)

        xs = _half_ffn(xs, m[6], m[7], m[8], f2_in, f2_out, *ln3, layer=l, alpha=alpha)
    return xs.reshape(batch, seq, d)
```

```python
import functools

import jax
import jax.numpy as jnp
from jax import lax
from jax.experimental import pallas as pl
from jax.experimental.pallas import tpu as pltpu

GRID_W = 64
N_LRU_HEADS = 4
RG_C = 8.0
CONV_SHORT = 4
SHORT_PAD_L = 2
CONV_K = 31
N_MOD = 9
ADA_ROWS = 8
EPS = 1e-6

SUBLANES = 8
LANES = 128
CONV_ROWS = 64
CONV_LANES = LANES
FFN_ROWS = 1024
FFN_TAIL_ROWS = 128
PROJ_COLS = 256
VMEM_LIMIT = 56 * 1024 * 1024

F32 = jnp.float32
BF16 = jnp.bfloat16


def _params(semantics):
    return pltpu.CompilerParams(dimension_semantics=semantics, vmem_limit_bytes=VMEM_LIMIT)


def _sigmoid(x):
    return 0.5 + 0.5 * jnp.tanh(0.5 * x)


def _layernorm(z, g, b):
    mu = jnp.mean(z, axis=-1, keepdims=True)
    d = z - mu
    var = jnp.mean(d * d, axis=-1, keepdims=True)
    return d * lax.rsqrt(var + EPS) * g + b


def _gelu_tanh(x):
    return 0.5 * x * (1.0 + jnp.tanh(0.7978845608028654 * (x + 0.044715 * (x * x * x))))


def _ada_kernel(c_ref, w_ref, b_ref, o_ref):
    c = c_ref[...]
    a = c * _sigmoid(c)
    o_ref[0] = jnp.dot(a, w_ref[0], preferred_element_type=F32) + b_ref[0]


def _ada(cond, w_ada, b_ada, tn=1024):
    depth, d, n = w_ada.shape
    return pl.pallas_call(
        _ada_kernel,
        grid=(depth, n // tn),
        in_specs=[
            pl.BlockSpec((ADA_ROWS, d), lambda l, j: (0, 0)),
            pl.BlockSpec((1, d, tn), lambda l, j: (l, 0, j)),
            pl.BlockSpec((1, 1, tn), lambda l, j: (l, 0, j)),
        ],
        out_specs=pl.BlockSpec((1, ADA_ROWS, tn), lambda l, j: (l, 0, j)),
        out_shape=jax.ShapeDtypeStruct((depth, ADA_ROWS, n), F32),
        compiler_params=_params(("parallel", "parallel")),
        name="ada",
    )(cond, w_ada, b_ada.reshape(depth, 1, n))


def _ffn_kernel(s_ref, sh_ref, sc_ref, gt_ref, w1g_ref, w1u_ref, w2_ref, g_ref, b_ref, o_ref, h_ref,
                *, alpha, n_chunks):
    j = pl.program_id(1)
    tm = s_ref.shape[0]
    assert n_chunks >= 2

    def chunk(first, last):
        if first:
            h = (s_ref[...] * (1.0 + sc_ref[0]) + sh_ref[0]).astype(BF16)
            h_ref[...] = h
        else:
            h = h_ref[...]
        gate = jnp.dot(h, w1g_ref[...], preferred_element_type=F32)
        up = jnp.dot(h, w1u_ref[...], preferred_element_type=F32)
        act = (gate * _sigmoid(gate) * up).astype(BF16)
        if first:
            o_ref[...] = jnp.dot(act, w2_ref[...], preferred_element_type=F32)
        elif not last:
            o_ref[...] += jnp.dot(act, w2_ref[...], preferred_element_type=F32)
        else:
            for r0 in range(0, tm, FFN_TAIL_ROWS):
                rows = slice(r0, r0 + FFN_TAIL_ROWS)
                y = o_ref[rows, :] + jnp.dot(act[rows], w2_ref[...], preferred_element_type=F32)
                z = alpha * s_ref[rows, :] + 0.5 * gt_ref[0] * y
                o_ref[rows, :] = _layernorm(z, g_ref[...], b_ref[...])

    pl.when(j == 0)(functools.partial(chunk, True, False))
    pl.when((j > 0) & (j < n_chunks - 1))(functools.partial(chunk, False, False))
    pl.when(j == n_chunks - 1)(functools.partial(chunk, False, True))


def _half_ffn(s, shift, scale, gate, w1, w2, g, b, *, layer, alpha, tf=512):
    n, d = s.shape
    d_ff = w2.shape[1]
    groups = shift.shape[0]
    tm = min(FFN_ROWS, n // groups)
    tiles_per_group = n // groups // tm
    n_chunks = d_ff // tf
    mod_spec = pl.BlockSpec((1, 1, d), lambda i, j: (i // tiles_per_group, 0, 0))
    vec_spec = pl.BlockSpec((1, d), lambda i, j: (0, 0))
    return pl.pallas_call(
        functools.partial(_ffn_kernel, alpha=alpha, n_chunks=n_chunks),
        grid=(n // tm, n_chunks),
        in_specs=[
            pl.BlockSpec((tm, d), lambda i, j: (i, 0)),
            mod_spec, mod_spec, mod_spec,
            pl.BlockSpec((None, d, tf), lambda i, j: (layer, 0, j)),
            pl.BlockSpec((None, d, tf), lambda i, j: (layer, 0, j + n_chunks)),
            pl.BlockSpec((None, tf, d), lambda i, j: (layer, j, 0)),
            vec_spec, vec_spec,
        ],
        out_specs=pl.BlockSpec((tm, d), lambda i, j: (i, 0)),
        out_shape=jax.ShapeDtypeStruct((n, d), F32),
        scratch_shapes=[pltpu.VMEM((tm, d), BF16)],
        compiler_params=_params(("parallel", "arbitrary")),
        name="half_ffn",
    )(s, shift, scale, gate, w1, w1, w2, g.reshape(1, d), b.reshape(1, d))


def _mix_in_kernel(x_ref, sh_ref, sc_ref, w_ref, *rest, seg, conv_out):
    if conv_out:
        c31w_ref, c31b_ref, cg_ref, cb_ref, xr_ref, gl_ref, yc_ref, upad_ref, conv_ref = rest
    else:
        (xr_ref,) = rest
    d_lru = xr_ref.shape[1]
    h = (x_ref[...] * (1.0 + sc_ref[0]) + sh_ref[0]).astype(BF16)
    if not conv_out:
        xr_ref[...] = jnp.dot(h, w_ref[...], preferred_element_type=F32)
        return
    d_conv = yc_ref.shape[1]
    tm = x_ref.shape[0]
    nseg = tm // seg
    half = CONV_K // 2
    lead = 2 * SUBLANES
    tail = upad_ref.shape[1] - lead - seg

    upad_ref[:, 0:lead, :] = jnp.zeros((nseg, lead, d_conv), F32)
    upad_ref[:, lead + seg:, :] = jnp.zeros((nseg, tail, d_conv), F32)

    def proj(c0):
        return jnp.dot(h, w_ref[:, c0:c0 + PROJ_COLS], preferred_element_type=F32)

    assert d_conv == d_lru
    for c0 in range(0, d_lru, PROJ_COLS):
        u = proj(2 * d_lru + c0) * _sigmoid(proj(2 * d_lru + d_conv + c0))
        upad_ref[:, lead:lead + seg, c0:c0 + PROJ_COLS] = u.reshape(nseg, seg, PROJ_COLS)
        xr_ref[:, c0:c0 + PROJ_COLS] = proj(c0)
        gl_ref[:, c0:c0 + PROJ_COLS] = _gelu_tanh(proj(d_lru + c0))

    offs = [lead - half + k for k in range(CONV_K)]
    slab_rows = CONV_ROWS + lead + tail
    for s in range(nseg):
        for rb in range(seg // CONV_ROWS):
            for cc in range(d_conv // CONV_LANES):
                cs = slice(cc * CONV_LANES, (cc + 1) * CONV_LANES)
                slab = upad_ref[s, rb * CONV_ROWS:rb * CONV_ROWS + slab_rows, cs]
                acc = jnp.broadcast_to(c31b_ref[:, cs], (CONV_ROWS, CONV_LANES))
                for r in range(SUBLANES):
                    rot = slab if r == 0 else pltpu.roll(slab, slab_rows - r, 0)
                    for k in range(CONV_K):
                        if offs[k] % SUBLANES == r:
                            q = offs[k] - r
                            assert q + CONV_ROWS <= slab_rows - r
                            acc = acc + c31w_ref[k:k + 1, cs] * rot[q:q + CONV_ROWS]
                row0 = s * seg + rb * CONV_ROWS
                conv_ref[row0:row0 + CONV_ROWS, cs] = acc
    y = _layernorm(conv_ref[...], cg_ref[...], cb_ref[...])
    yc_ref[...] = (y * _sigmoid(y)).astype(BF16)


def _mix_in(x, shift, scale, w_in, c31w, c31b, cg, cb, *, layer, seg, d_lru, conv_out, tm=512):
    n, d = x.shape
    d_in = w_in.shape[2]
    d_conv = (d_in - 2 * d_lru) // 2
    groups = shift.shape[0]
    tiles_per_group = n // groups // tm
    mod_spec = pl.BlockSpec((1, 1, d), lambda i: (i // tiles_per_group, 0, 0))
    row_lru = pl.BlockSpec((tm, d_lru), lambda i: (i, 0))
    in_specs = [pl.BlockSpec((tm, d), lambda i: (i, 0)), mod_spec, mod_spec]
    args = [x, shift, scale]
    if conv_out:
        in_specs += [
            pl.BlockSpec((None, d, d_in), lambda i: (layer, 0, 0)),
            pl.BlockSpec((CONV_K, d_conv), lambda i: (0, 0)),
            pl.BlockSpec((1, d_conv), lambda i: (0, 0)),
            pl.BlockSpec((1, d_conv), lambda i: (0, 0)),
            pl.BlockSpec((1, d_conv), lambda i: (0, 0)),
        ]
        args += [w_in, c31w, c31b.reshape(1, d_conv), cg.reshape(1, d_conv), cb.reshape(1, d_conv)]
        out_specs = [row_lru, row_lru, pl.BlockSpec((tm, d_conv), lambda i: (i, 0))]
        out_shape = [jax.ShapeDtypeStruct((n, d_lru), F32), jax.ShapeDtypeStruct((n, d_lru), F32),
                     jax.ShapeDtypeStruct((n, d_conv), BF16)]
        pad_rows = 2 * SUBLANES + seg + 2 * SUBLANES
        scratch = [pltpu.VMEM((tm // seg, pad_rows, d_conv), F32), pltpu.VMEM((tm, d_conv), F32)]
    else:
        in_specs += [pl.BlockSpec((None, d, d_lru), lambda i: (layer, 0, 0))]
        args += [w_in]
        out_specs = [row_lru]
        out_shape = [jax.ShapeDtypeStruct((n, d_lru), F32)]
        scratch = []
    return pl.pallas_call(
        functools.partial(_mix_in_kernel, seg=seg, conv_out=conv_out),
        grid=(n // tm,),
        in_specs=in_specs,
        out_specs=out_specs,
        out_shape=out_shape,
        scratch_shapes=scratch,
        compiler_params=_params(("parallel",)),
        name="mix_in",
    )(*args)


def _scan_kernel(xr_ref, xp_ref, xn_ref, c4w_ref, c4b_ref, wr_ref, br_ref, wi_ref, bi_ref,
                 lam_ref, h0_ref, *rest, reverse, tiles_per_seq, n_tiles, fuse_out):
    if fuse_out:
        hb_ref, gl_ref, o_ref, hs_ref, ext_ref, a_ref, b_ref, carry_ref = rest
    else:
        hs_ref, ext_ref, a_ref, b_ref, carry_ref = rest
    i = pl.program_id(0)
    t = (n_tiles - 1 - i) if reverse else i
    p = t % tiles_per_seq
    first = p == 0
    last = p == tiles_per_seq - 1
    tm, c = xr_ref.shape
    hd = c // N_LRU_HEADS

    ext_ref[0:SUBLANES, :] = jnp.where(first, 0.0, xp_ref[...])
    ext_ref[SUBLANES:SUBLANES + tm, :] = xr_ref[...]
    ext_ref[SUBLANES + tm:, :] = jnp.where(last, 0.0, xn_ref[...])
    xc = jnp.zeros((tm, c), F32) + c4b_ref[...]
    for k in range(CONV_SHORT):
        off = SUBLANES - SHORT_PAD_L + k
        xc = xc + c4w_ref[k:k + 1, :] * ext_ref[off:off + tm, :]

    xcb = xc.astype(BF16)
    lam = lam_ref[...]
    softplus_neg_lam = jnp.maximum(-lam, 0.0) + jnp.log(1.0 + jnp.exp(-jnp.abs(lam)))
    coef = -RG_C * softplus_neg_lam
    for hh in range(N_LRU_HEADS):
        cs = slice(hh * hd, (hh + 1) * hd)
        xh = xcb[:, cs]
        r = _sigmoid(jnp.dot(xh, wr_ref[hh], preferred_element_type=F32) + br_ref[:, cs])
        ig = _sigmoid(jnp.dot(xh, wi_ref[hh], preferred_element_type=F32) + bi_ref[:, cs])
        log_a = coef[:, cs] * r
        a = jnp.exp(log_a)
        a_ref[:, cs] = a
        b_ref[:, cs] = jnp.sqrt(1.0 - jnp.exp(2.0 * log_a)) * (ig * xc[:, cs])

    start = last if reverse else first

    @pl.when(start)
    def _():
        carry_ref[...] = h0_ref[0]

    n_groups = tm // SUBLANES
    row = lax.broadcasted_iota(jnp.int32, (SUBLANES, c), 0)

    def group(k, carry):
        kk = (n_groups - 1 - k) if reverse else k
        r0 = pl.multiple_of(kk * SUBLANES, SUBLANES)
        a = a_ref[pl.ds(r0, SUBLANES), :]
        b = b_ref[pl.ds(r0, SUBLANES), :]
        for s in (1, 2, 4):
            if reverse:
                shift, keep = SUBLANES - s, row < SUBLANES - s
            else:
                shift, keep = s, row >= s
            a_prev = jnp.where(keep, pltpu.roll(a, shift, 0), 1.0)
            b_prev = jnp.where(keep, pltpu.roll(b, shift, 0), 0.0)
            b = b + a * b_prev
            a = a * a_prev
        h = b + a * carry
        hs_ref[pl.ds(r0, SUBLANES), :] = h
        return h[0:1, :] if reverse else h[SUBLANES - 1:SUBLANES, :]

    carry_ref[...] = lax.fori_loop(0, n_groups, group, carry_ref[...])

    if fuse_out:
        o_ref[...] = ((hs_ref[...] + hb_ref[...]) * gl_ref[...]).astype(BF16)


def _scan(xr, c4w, c4b, wr, br, wi, bi, lam, h0, *, layer, direction, seq_len, other=None, gelu=None,
          want_states=True, tm=512):
    reverse = direction == 1
    n, c = xr.shape
    tm = min(tm, seq_len)
    n_tiles = n // tm
    tiles_per_seq = seq_len // tm
    hd = c // N_LRU_HEADS
    fuse_out = other is not None
    blocks_per_tile = tm // SUBLANES
    n_blocks = n // SUBLANES

    def tile(i):
        return (n_tiles - 1 - i) if reverse else i

    row_spec = pl.BlockSpec((tm, c), lambda i: (tile(i), 0))
    vec_spec = pl.BlockSpec((1, c), lambda i: (0, 0))
    w_spec = pl.BlockSpec((None, None, N_LRU_HEADS, hd, hd), lambda i: (layer, direction, 0, 0, 0))
    in_specs = [
        row_spec,
        pl.BlockSpec((SUBLANES, c), lambda i: (jnp.maximum(tile(i) * blocks_per_tile - 1, 0), 0)),
        pl.BlockSpec((SUBLANES, c), lambda i: (jnp.minimum((tile(i) + 1) * blocks_per_tile, n_blocks - 1), 0)),
        pl.BlockSpec((CONV_SHORT, c), lambda i: (0, 0)),
        vec_spec, w_spec, vec_spec, w_spec, vec_spec, vec_spec,
        pl.BlockSpec((1, 1, c), lambda i: (tile(i) // tiles_per_seq, 0, 0)),
    ]
    args = [xr, xr, xr, c4w, c4b.reshape(1, c), wr, br.reshape(1, c), wi, bi.reshape(1, c),
            lam.reshape(1, c), h0]
    scratch = [pltpu.VMEM((tm + 2 * SUBLANES, c), F32), pltpu.VMEM((tm, c), F32),
               pltpu.VMEM((tm, c), F32), pltpu.VMEM((1, c), F32)]
    states = jax.ShapeDtypeStruct((n, c), F32)
    fused = jax.ShapeDtypeStruct((n, c), BF16)
    if fuse_out:
        in_specs += [row_spec, row_spec]
        args += [other, gelu]
    if fuse_out and want_states:
        out_specs, out_shape = [row_spec, row_spec], [fused, states]
    elif fuse_out:
        out_specs, out_shape = row_spec, fused
        scratch = [pltpu.VMEM((tm, c), F32)] + scratch
    else:
        out_specs, out_shape = row_spec, states
    return pl.pallas_call(
        functools.partial(_scan_kernel, reverse=reverse, tiles_per_seq=tiles_per_seq,
                          n_tiles=n_tiles, fuse_out=fuse_out),
        grid=(n_tiles,),
        in_specs=in_specs,
        out_specs=out_specs,
        out_shape=out_shape,
        scratch_shapes=scratch,
        compiler_params=_params(("arbitrary",)),
        name="lru_scan",
    )(*args)


def _mix_out_kernel(x_ref, gt_ref, yr_ref, yc_ref, wa_ref, wb_ref, bo_ref, g_ref, b_ref, o_ref, *, alpha):
    y = jnp.dot(yr_ref[...], wa_ref[...], preferred_element_type=F32)
    y = y + jnp.dot(yc_ref[...], wb_ref[...], preferred_element_type=F32) + bo_ref[...]
    z = alpha * x_ref[...] + gt_ref[0] * y
    o_ref[...] = _layernorm(z, g_ref[...], b_ref[...])


def _mix_out(x, gate, y_rec, y_conv, w_out, b_out, g, b, *, layer, alpha, tm=512):
    n, d = x.shape
    d_lru = y_rec.shape[1]
    d_conv = y_conv.shape[1]
    groups = gate.shape[0]
    tiles_per_group = n // groups // tm
    vec_spec = pl.BlockSpec((1, d), lambda i: (0, 0))
    assert d_lru == d_conv
    return pl.pallas_call(
        functools.partial(_mix_out_kernel, alpha=alpha),
        grid=(n // tm,),
        in_specs=[
            pl.BlockSpec((tm, d), lambda i: (i, 0)),
            pl.BlockSpec((1, 1, d), lambda i: (i // tiles_per_group, 0, 0)),
            pl.BlockSpec((tm, d_lru), lambda i: (i, 0)),
            pl.BlockSpec((tm, d_conv), lambda i: (i, 0)),
            pl.BlockSpec((None, d_lru, d), lambda i: (layer, 0, 0)),
            pl.BlockSpec((None, d_conv, d), lambda i: (layer, 1, 0)),
            vec_spec, vec_spec, vec_spec,
        ],
        out_specs=pl.BlockSpec((tm, d), lambda i: (i, 0)),
        out_shape=jax.ShapeDtypeStruct((n, d), F32),
        compiler_params=_params(("parallel",)),
        name="mix_out",
    )(x, gate, y_rec, y_conv, w_out, w_out, b_out.reshape(1, d), g.reshape(1, d), b.reshape(1, d))


def kernel(x, c, ctx, c_ctx, w_ada, b_ada, ln_g, ln_b, ff1_in, ff1_out, ff2_in, ff2_out, w_in, conv4_w,
           conv4_b, w_rg, b_rg, w_ig, b_ig, lam, conv31_w, conv31_b, cln_g, cln_b, w_out, b_out):
    batch, seq, d = x.shape
    ctx_len = ctx.shape[1]
    depth = w_ada.shape[0]
    d_lru = conv4_w.shape[-1]
    alpha = (2 * depth) ** 0.25

    cond = jnp.concatenate([c, c_ctx[None, :], jnp.zeros((ADA_ROWS - batch - 1, d), F32)], axis=0)
    mods = _ada(cond, w_ada, b_ada).reshape(depth, ADA_ROWS, N_MOD, d)

    f1_in, f1_out = ff1_in.astype(BF16), ff1_out.astype(BF16)
    w_out_b = w_out.astype(BF16)
    wr, wi = w_rg.astype(BF16), w_ig.astype(BF16)
    f2_in, f2_out, w_in_b = ff2_in.astype(BF16), ff2_out.astype(BF16), w_in.astype(BF16)

    xs = x.reshape(batch * seq, d)
    cs = ctx.reshape(batch * ctx_len, d)
    zeros_h0 = jnp.zeros((batch, 1, d_lru), F32)
    for l in range(depth):
        last = l == depth - 1
        m = [mods[l, 0:batch, k][:, None, :] for k in range(N_MOD)]
        mc = [mods[l, batch:batch + 1, k][:, None, :] for k in range(N_MOD)]
        conv = (conv31_w[l], conv31_b[l], cln_g[l], cln_b[l])

        def scan(xr_, h0_, direction, seq_len, **kw):
            return _scan(xr_, conv4_w[l], conv4_b[l], wr, b_rg[l, direction], wi, b_ig[l, direction],
                         lam[l, direction], h0_, layer=l, direction=direction, seq_len=seq_len, **kw)

        ln1 = (ln_g[l, 0], ln_b[l, 0])
        xs = _half_ffn(xs, m[0], m[1], m[2], f1_in, f1_out, *ln1, layer=l, alpha=alpha)
        cs = _half_ffn(cs, mc[0], mc[1], mc[2], f1_in, f1_out, *ln1, layer=l, alpha=alpha)

        if not last:
            xr_c, gl_c, yc_c = _mix_in(cs, mc[3], mc[4], w_in_b, *conv, layer=l, seg=ctx_len, d_lru=d_lru,
                                       conv_out=True)
        else:
            (xr_c,) = _mix_in(cs, mc[3], mc[4], w_in_b, *conv, layer=l, seg=ctx_len, d_lru=d_lru,
                              conv_out=False)
        hb_c = scan(xr_c, zeros_h0, 1, ctx_len)
        if not last:
            yr_c, hf_c = scan(xr_c, zeros_h0, 0, ctx_len, other=hb_c, gelu=gl_c)
        else:
            hf_c = scan(xr_c, zeros_h0, 0, ctx_len)
        h0_b = hb_c.reshape(batch, ctx_len, d_lru)[:, 0:1, :]
        h0_f = hf_c.reshape(batch, ctx_len, d_lru)[:, ctx_len - 1:ctx_len, :]

        xr, gl, yc = _mix_in(xs, m[3], m[4], w_in_b, *conv, layer=l, seg=GRID_W, d_lru=d_lru, conv_out=True)
        hb = scan(xr, h0_b, 1, seq)
        yr = scan(xr, h0_f, 0, seq, other=hb, gelu=gl, want_states=False)
        ln2 = (ln_g[l, 1], ln_b[l, 1])
        xs = _mix_out(xs, m[5], yr, yc, w_out_b, b_out[l], *ln2, layer=l, alpha=alpha)
        ln3 = (ln_g[l, 2], ln_b[l, 2])
        if not last:
            cs = _mix_out(cs, mc[5], yr_c, yc_c, w_out_b, b_out[l], *ln2, layer=l, alpha=alpha)
            cs = _half_ffn(cs, mc[6], mc[7], mc[8], f2_in, f2_out, *ln3, layer=l, alpha=alpha)

        xs = _half_ffn(xs, m[6], m[7], m[8], f2_in, f2_out, *ln3, layer=l, alpha=alpha)
    return xs.reshape(batch, seq, d)
```

```python
import functools

import jax
import jax.numpy as jnp
from jax import lax
from jax.experimental import pallas as pl
from jax.experimental.pallas import tpu as pltpu

GRID_W = 64
N_LRU_HEADS = 4
RG_C = 8.0
CONV_SHORT = 4
SHORT_PAD_L = 2
CONV_K = 31
N_MOD = 9
ADA_ROWS = 8
EPS = 1e-6

SUBLANES = 8
LANES = 128
CONV_ROWS = 64
CONV_LANES = LANES
FFN_ROWS = 1024
FFN_COLS = 512
FFN_CAST_COLS = 256
FFN_TAIL_ROWS = 128
PROJ_COLS = 256
VMEM_LIMIT = 56 * 1024 * 1024

F32 = jnp.float32
BF16 = jnp.bfloat16


def _params(semantics):
    return pltpu.CompilerParams(dimension_semantics=semantics, vmem_limit_bytes=VMEM_LIMIT)


def _sigmoid(x):
    return 0.5 + 0.5 * jnp.tanh(0.5 * x)


def _layernorm(z, g, b):
    mu = jnp.mean(z, axis=-1, keepdims=True)
    d = z - mu
    var = jnp.mean(d * d, axis=-1, keepdims=True)
    return d * lax.rsqrt(var + EPS) * g + b


def _gelu_tanh(x):
    return 0.5 * x * (1.0 + jnp.tanh(0.7978845608028654 * (x + 0.044715 * (x * x * x))))


def _ada_kernel(c_ref, w_ref, b_ref, o_ref):
    c = c_ref[...]
    a = c * _sigmoid(c)
    o_ref[0] = jnp.dot(a, w_ref[0], preferred_element_type=F32) + b_ref[0]


def _ada(cond, w_ada, b_ada, tn=1024):
    depth, d, n = w_ada.shape
    return pl.pallas_call(
        _ada_kernel,
        grid=(depth, n // tn),
        in_specs=[
            pl.BlockSpec((ADA_ROWS, d), lambda l, j: (0, 0)),
            pl.BlockSpec((1, d, tn), lambda l, j: (l, 0, j)),
            pl.BlockSpec((1, 1, tn), lambda l, j: (l, 0, j)),
        ],
        out_specs=pl.BlockSpec((1, ADA_ROWS, tn), lambda l, j: (l, 0, j)),
        out_shape=jax.ShapeDtypeStruct((depth, ADA_ROWS, n), F32),
        compiler_params=_params(("parallel", "parallel")),
        name="ada",
    )(cond, w_ada, b_ada.reshape(depth, 1, n))


def _ffn_kernel(s_ref, sh_ref, sc_ref, gt_ref, w1g_ref, w1u_ref, w2_ref, g_ref, b_ref, o_ref, *rest,
                alpha, n_chunks, emit_weights):
    if emit_weights:
        w1g_out, w1u_out, w2_out, h_ref = rest
    else:
        (h_ref,) = rest
    j = pl.program_id(1)
    tm = s_ref.shape[0]
    assert n_chunks >= 2

    def weights():
        if not emit_weights:
            return w1g_ref[...], w1u_ref[...], w2_ref[...]
        ws = [ref[...].astype(BF16) for ref in (w1g_ref, w1u_ref, w2_ref)]
        for out_ref, w in zip((w1g_out, w1u_out, w2_out), ws):
            out_ref[...] = w
        return ws

    def chunk(first, last):
        w1g, w1u, w2 = weights()
        if first:
            h = (s_ref[...] * (1.0 + sc_ref[0]) + sh_ref[0]).astype(BF16)
            h_ref[...] = h
        else:
            h = h_ref[...]
        gate = jnp.dot(h, w1g, preferred_element_type=F32)
        up = jnp.dot(h, w1u, preferred_element_type=F32)
        act = (gate * _sigmoid(gate) * up).astype(BF16)
        if first:
            o_ref[...] = jnp.dot(act, w2, preferred_element_type=F32)
        elif not last:
            o_ref[...] += jnp.dot(act, w2, preferred_element_type=F32)
        else:
            for r0 in range(0, tm, FFN_TAIL_ROWS):
                rows = slice(r0, r0 + FFN_TAIL_ROWS)
                y = o_ref[rows, :] + jnp.dot(act[rows], w2, preferred_element_type=F32)
                z = alpha * s_ref[rows, :] + 0.5 * gt_ref[0] * y
                o_ref[rows, :] = _layernorm(z, g_ref[...], b_ref[...])

    pl.when(j == 0)(functools.partial(chunk, True, False))
    pl.when((j > 0) & (j < n_chunks - 1))(functools.partial(chunk, False, False))
    pl.when(j == n_chunks - 1)(functools.partial(chunk, False, True))


def _ffn_weight_specs(d, d_ff, tf, layer):
    n_chunks = d_ff // tf
    if layer is None:
        return [pl.BlockSpec((d, tf), lambda *ij: (0, ij[-1])),
                pl.BlockSpec((d, tf), lambda *ij: (0, ij[-1])),
                pl.BlockSpec((tf, d), lambda *ij: (ij[-1], 0))]
    return [pl.BlockSpec((None, d, tf), lambda *ij: (layer, 0, ij[-1])),
            pl.BlockSpec((None, d, tf), lambda *ij: (layer, 0, ij[-1] + n_chunks)),
            pl.BlockSpec((None, tf, d), lambda *ij: (layer, ij[-1], 0))]


def _ffn_weight_shapes(d, d_ff):
    return [jax.ShapeDtypeStruct((d, d_ff), BF16), jax.ShapeDtypeStruct((d, d_ff), BF16),
            jax.ShapeDtypeStruct((d_ff, d), BF16)]


def _half_ffn(s, shift, scale, gate, g, b, *, alpha, w_bf16=None, w_f32=None):
    n, d = s.shape
    emit = w_f32 is not None
    if emit:
        w1, w2, layer = w_f32
        d_ff, tf, w_args = w2.shape[1], FFN_CAST_COLS, (w1, w1, w2)
    else:
        layer, d_ff, tf, w_args = None, w_bf16[2].shape[0], FFN_COLS, w_bf16
    groups = shift.shape[0]
    tm = min(FFN_ROWS, n // groups)
    tiles_per_group = n // groups // tm
    n_chunks = d_ff // tf
    mod_spec = pl.BlockSpec((1, 1, d), lambda i, j: (i // tiles_per_group, 0, 0))
    vec_spec = pl.BlockSpec((1, d), lambda i, j: (0, 0))
    row_spec = pl.BlockSpec((tm, d), lambda i, j: (i, 0))
    out_specs, out_shape = [row_spec], [jax.ShapeDtypeStruct((n, d), F32)]
    if emit:
        assert n == tm, "each weight chunk must be emitted exactly once"
        out_specs += _ffn_weight_specs(d, d_ff, tf, None)
        out_shape += _ffn_weight_shapes(d, d_ff)
    out = pl.pallas_call(
        functools.partial(_ffn_kernel, alpha=alpha, n_chunks=n_chunks, emit_weights=emit),
        grid=(n // tm, n_chunks),
        in_specs=[row_spec, mod_spec, mod_spec, mod_spec] + _ffn_weight_specs(d, d_ff, tf, layer)
        + [vec_spec, vec_spec],
        out_specs=out_specs,
        out_shape=out_shape,
        scratch_shapes=[pltpu.VMEM((tm, d), BF16)],
        compiler_params=_params(("parallel", "arbitrary")),
        name="half_ffn",
    )(s, shift, scale, gate, *w_args, g.reshape(1, d), b.reshape(1, d))
    return (out[0], tuple(out[1:])) if emit else out[0]


def _cast_kernel(w1g_ref, w1u_ref, w2_ref, w1g_out, w1u_out, w2_out):
    for src_ref, dst_ref in ((w1g_ref, w1g_out), (w1u_ref, w1u_out), (w2_ref, w2_out)):
        dst_ref[...] = src_ref[...].astype(BF16)


def _cast_ffn_weights(w1, w2, layer, tf=512):
    d, d_ff = w1.shape[1], w2.shape[1]
    return tuple(pl.pallas_call(
        _cast_kernel,
        grid=(d_ff // tf,),
        in_specs=_ffn_weight_specs(d, d_ff, tf, layer),
        out_specs=_ffn_weight_specs(d, d_ff, tf, None),
        out_shape=_ffn_weight_shapes(d, d_ff),
        compiler_params=_params(("parallel",)),
        name="cast_ffn_weights",
    )(w1, w1, w2))


def _mix_in_kernel(x_ref, sh_ref, sc_ref, w_ref, *rest, seg, conv_out):
    if conv_out:
        c31w_ref, c31b_ref, cg_ref, cb_ref, xr_ref, gl_ref, yc_ref, upad_ref, conv_ref = rest
    else:
        (xr_ref,) = rest
    d_lru = xr_ref.shape[1]
    h = (x_ref[...] * (1.0 + sc_ref[0]) + sh_ref[0]).astype(BF16)
    if not conv_out:
        xr_ref[...] = jnp.dot(h, w_ref[...], preferred_element_type=F32)
        return
    d_conv = yc_ref.shape[1]
    tm = x_ref.shape[0]
    nseg = tm // seg
    half = CONV_K // 2
    lead = 2 * SUBLANES
    tail = upad_ref.shape[1] - lead - seg

    upad_ref[:, 0:lead, :] = jnp.zeros((nseg, lead, d_conv), F32)
    upad_ref[:, lead + seg:, :] = jnp.zeros((nseg, tail, d_conv), F32)

    def proj(c0):
        return jnp.dot(h, w_ref[:, c0:c0 + PROJ_COLS], preferred_element_type=F32)

    assert d_conv == d_lru
    for c0 in range(0, d_lru, PROJ_COLS):
        u = proj(2 * d_lru + c0) * _sigmoid(proj(2 * d_lru + d_conv + c0))
        upad_ref[:, lead:lead + seg, c0:c0 + PROJ_COLS] = u.reshape(nseg, seg, PROJ_COLS)
        xr_ref[:, c0:c0 + PROJ_COLS] = proj(c0)
        gl_ref[:, c0:c0 + PROJ_COLS] = _gelu_tanh(proj(d_lru + c0))

    offs = [lead - half + k for k in range(CONV_K)]
    slab_rows = CONV_ROWS + lead + tail
    for s in range(nseg):
        for rb in range(seg // CONV_ROWS):
            for cc in range(d_conv // CONV_LANES):
                cs = slice(cc * CONV_LANES, (cc + 1) * CONV_LANES)
                slab = upad_ref[s, rb * CONV_ROWS:rb * CONV_ROWS + slab_rows, cs]
                acc = jnp.broadcast_to(c31b_ref[:, cs], (CONV_ROWS, CONV_LANES))
                for r in range(SUBLANES):
                    rot = slab if r == 0 else pltpu.roll(slab, slab_rows - r, 0)
                    for k in range(CONV_K):
                        if offs[k] % SUBLANES == r:
                            q = offs[k] - r
                            assert q + CONV_ROWS <= slab_rows - r
                            acc = acc + c31w_ref[k:k + 1, cs] * rot[q:q + CONV_ROWS]
                row0 = s * seg + rb * CONV_ROWS
                conv_ref[row0:row0 + CONV_ROWS, cs] = acc
    y = _layernorm(conv_ref[...], cg_ref[...], cb_ref[...])
    yc_ref[...] = (y * _sigmoid(y)).astype(BF16)


def _mix_in(x, shift, scale, w_in, c31w, c31b, cg, cb, *, layer, seg, d_lru, conv_out, tm=512):
    n, d = x.shape
    d_in = w_in.shape[2]
    d_conv = (d_in - 2 * d_lru) // 2
    groups = shift.shape[0]
    tiles_per_group = n // groups // tm
    mod_spec = pl.BlockSpec((1, 1, d), lambda i: (i // tiles_per_group, 0, 0))
    row_lru = pl.BlockSpec((tm, d_lru), lambda i: (i, 0))
    in_specs = [pl.BlockSpec((tm, d), lambda i: (i, 0)), mod_spec, mod_spec]
    args = [x, shift, scale]
    if conv_out:
        in_specs += [
            pl.BlockSpec((None, d, d_in), lambda i: (layer, 0, 0)),
            pl.BlockSpec((CONV_K, d_conv), lambda i: (0, 0)),
            pl.BlockSpec((1, d_conv), lambda i: (0, 0)),
            pl.BlockSpec((1, d_conv), lambda i: (0, 0)),
            pl.BlockSpec((1, d_conv), lambda i: (0, 0)),
        ]
        args += [w_in, c31w, c31b.reshape(1, d_conv), cg.reshape(1, d_conv), cb.reshape(1, d_conv)]
        out_specs = [row_lru, row_lru, pl.BlockSpec((tm, d_conv), lambda i: (i, 0))]
        out_shape = [jax.ShapeDtypeStruct((n, d_lru), F32), jax.ShapeDtypeStruct((n, d_lru), F32),
                     jax.ShapeDtypeStruct((n, d_conv), BF16)]
        pad_rows = 2 * SUBLANES + seg + 2 * SUBLANES
        scratch = [pltpu.VMEM((tm // seg, pad_rows, d_conv), F32), pltpu.VMEM((tm, d_conv), F32)]
    else:
        in_specs += [pl.BlockSpec((None, d, d_lru), lambda i: (layer, 0, 0))]
        args += [w_in]
        out_specs = [row_lru]
        out_shape = [jax.ShapeDtypeStruct((n, d_lru), F32)]
        scratch = []
    return pl.pallas_call(
        functools.partial(_mix_in_kernel, seg=seg, conv_out=conv_out),
        grid=(n // tm,),
        in_specs=in_specs,
        out_specs=out_specs,
        out_shape=out_shape,
        scratch_shapes=scratch,
        compiler_params=_params(("parallel",)),
        name="mix_in",
    )(*args)


def _scan_kernel(xr_ref, xp_ref, xn_ref, c4w_ref, c4b_ref, wr_ref, br_ref, wi_ref, bi_ref,
                 lam_ref, h0_ref, *rest, reverse, tiles_per_seq, n_tiles, fuse_out):
    if fuse_out:
        hb_ref, gl_ref, o_ref, hs_ref, ext_ref, a_ref, b_ref, carry_ref = rest
    else:
        hs_ref, ext_ref, a_ref, b_ref, carry_ref = rest
    i = pl.program_id(0)
    t = (n_tiles - 1 - i) if reverse else i
    p = t % tiles_per_seq
    first = p == 0
    last = p == tiles_per_seq - 1
    tm, c = xr_ref.shape
    hd = c // N_LRU_HEADS

    ext_ref[0:SUBLANES, :] = jnp.where(first, 0.0, xp_ref[...])
    ext_ref[SUBLANES:SUBLANES + tm, :] = xr_ref[...]
    ext_ref[SUBLANES + tm:, :] = jnp.where(last, 0.0, xn_ref[...])
    xc = jnp.zeros((tm, c), F32) + c4b_ref[...]
    for k in range(CONV_SHORT):
        off = SUBLANES - SHORT_PAD_L + k
        xc = xc + c4w_ref[k:k + 1, :] * ext_ref[off:off + tm, :]

    xcb = xc.astype(BF16)
    lam = lam_ref[...]
    softplus_neg_lam = jnp.maximum(-lam, 0.0) + jnp.log(1.0 + jnp.exp(-jnp.abs(lam)))
    coef = -RG_C * softplus_neg_lam
    for hh in range(N_LRU_HEADS):
        cs = slice(hh * hd, (hh + 1) * hd)
        xh = xcb[:, cs]
        r = _sigmoid(jnp.dot(xh, wr_ref[hh], preferred_element_type=F32) + br_ref[:, cs])
        ig = _sigmoid(jnp.dot(xh, wi_ref[hh], preferred_element_type=F32) + bi_ref[:, cs])
        log_a = coef[:, cs] * r
        a = jnp.exp(log_a)
        a_ref[:, cs] = a
        b_ref[:, cs] = jnp.sqrt(1.0 - jnp.exp(2.0 * log_a)) * (ig * xc[:, cs])

    start = last if reverse else first

    @pl.when(start)
    def _():
        carry_ref[...] = h0_ref[0]

    n_groups = tm // SUBLANES
    row = lax.broadcasted_iota(jnp.int32, (SUBLANES, c), 0)

    def group(k, carry):
        kk = (n_groups - 1 - k) if reverse else k
        r0 = pl.multiple_of(kk * SUBLANES, SUBLANES)
        a = a_ref[pl.ds(r0, SUBLANES), :]
        b = b_ref[pl.ds(r0, SUBLANES), :]
        for s in (1, 2, 4):
            if reverse:
                shift, keep = SUBLANES - s, row < SUBLANES - s
            else:
                shift, keep = s, row >= s
            a_prev = jnp.where(keep, pltpu.roll(a, shift, 0), 1.0)
            b_prev = jnp.where(keep, pltpu.roll(b, shift, 0), 0.0)
            b = b + a * b_prev
            a = a * a_prev
        h = b + a * carry
        hs_ref[pl.ds(r0, SUBLANES), :] = h
        return h[0:1, :] if reverse else h[SUBLANES - 1:SUBLANES, :]

    carry_ref[...] = lax.fori_loop(0, n_groups, group, carry_ref[...])

    if fuse_out:
        o_ref[...] = ((hs_ref[...] + hb_ref[...]) * gl_ref[...]).astype(BF16)


def _scan(xr, c4w, c4b, wr, br, wi, bi, lam, h0, *, layer, direction, seq_len, other=None, gelu=None,
          want_states=True, tm=512):
    reverse = direction == 1
    n, c = xr.shape
    tm = min(tm, seq_len)
    n_tiles = n // tm
    tiles_per_seq = seq_len // tm
    hd = c // N_LRU_HEADS
    fuse_out = other is not None
    blocks_per_tile = tm // SUBLANES
    n_blocks = n // SUBLANES

    def tile(i):
        return (n_tiles - 1 - i) if reverse else i

    row_spec = pl.BlockSpec((tm, c), lambda i: (tile(i), 0))
    vec_spec = pl.BlockSpec((1, c), lambda i: (0, 0))
    w_spec = pl.BlockSpec((None, None, N_LRU_HEADS, hd, hd), lambda i: (layer, direction, 0, 0, 0))
    in_specs = [
        row_spec,
        pl.BlockSpec((SUBLANES, c), lambda i: (jnp.maximum(tile(i) * blocks_per_tile - 1, 0), 0)),
        pl.BlockSpec((SUBLANES, c), lambda i: (jnp.minimum((tile(i) + 1) * blocks_per_tile, n_blocks - 1), 0)),
        pl.BlockSpec((CONV_SHORT, c), lambda i: (0, 0)),
        vec_spec, w_spec, vec_spec, w_spec, vec_spec, vec_spec,
        pl.BlockSpec((1, 1, c), lambda i: (tile(i) // tiles_per_seq, 0, 0)),
    ]
    args = [xr, xr, xr, c4w, c4b.reshape(1, c), wr, br.reshape(1, c), wi, bi.reshape(1, c),
            lam.reshape(1, c), h0]
    scratch = [pltpu.VMEM((tm + 2 * SUBLANES, c), F32), pltpu.VMEM((tm, c), F32),
               pltpu.VMEM((tm, c), F32), pltpu.VMEM((1, c), F32)]
    states = jax.ShapeDtypeStruct((n, c), F32)
    fused = jax.ShapeDtypeStruct((n, c), BF16)
    if fuse_out:
        in_specs += [row_spec, row_spec]
        args += [other, gelu]
    if fuse_out and want_states:
        out_specs, out_shape = [row_spec, row_spec], [fused, states]
    elif fuse_out:
        out_specs, out_shape = row_spec, fused
        scratch = [pltpu.VMEM((tm, c), F32)] + scratch
    else:
        out_specs, out_shape = row_spec, states
    return pl.pallas_call(
        functools.partial(_scan_kernel, reverse=reverse, tiles_per_seq=tiles_per_seq,
                          n_tiles=n_tiles, fuse_out=fuse_out),
        grid=(n_tiles,),
        in_specs=in_specs,
        out_specs=out_specs,
        out_shape=out_shape,
        scratch_shapes=scratch,
        compiler_params=_params(("arbitrary",)),
        name="lru_scan",
    )(*args)


def _mix_out_kernel(x_ref, gt_ref, yr_ref, yc_ref, wa_ref, wb_ref, bo_ref, g_ref, b_ref, o_ref, *, alpha):
    y = jnp.dot(yr_ref[...], wa_ref[...], preferred_element_type=F32)
    y = y + jnp.dot(yc_ref[...], wb_ref[...], preferred_element_type=F32) + bo_ref[...]
    z = alpha * x_ref[...] + gt_ref[0] * y
    o_ref[...] = _layernorm(z, g_ref[...], b_ref[...])


def _mix_out(x, gate, y_rec, y_conv, w_out, b_out, g, b, *, layer, alpha, tm=512):
    n, d = x.shape
    d_lru = y_rec.shape[1]
    d_conv = y_conv.shape[1]
    groups = gate.shape[0]
    tiles_per_group = n // groups // tm
    vec_spec = pl.BlockSpec((1, d), lambda i: (0, 0))
    assert d_lru == d_conv
    return pl.pallas_call(
        functools.partial(_mix_out_kernel, alpha=alpha),
        grid=(n // tm,),
        in_specs=[
            pl.BlockSpec((tm, d), lambda i: (i, 0)),
            pl.BlockSpec((1, 1, d), lambda i: (i // tiles_per_group, 0, 0)),
            pl.BlockSpec((tm, d_lru), lambda i: (i, 0)),
            pl.BlockSpec((tm, d_conv), lambda i: (i, 0)),
            pl.BlockSpec((None, d_lru, d), lambda i: (layer, 0, 0)),
            pl.BlockSpec((None, d_conv, d), lambda i: (layer, 1, 0)),
            vec_spec, vec_spec, vec_spec,
        ],
        out_specs=pl.BlockSpec((tm, d), lambda i: (i, 0)),
        out_shape=jax.ShapeDtypeStruct((n, d), F32),
        compiler_params=_params(("parallel",)),
        name="mix_out",
    )(x, gate, y_rec, y_conv, w_out, w_out, b_out.reshape(1, d), g.reshape(1, d), b.reshape(1, d))


def kernel(x, c, ctx, c_ctx, w_ada, b_ada, ln_g, ln_b, ff1_in, ff1_out, ff2_in, ff2_out, w_in, conv4_w,
           conv4_b, w_rg, b_rg, w_ig, b_ig, lam, conv31_w, conv31_b, cln_g, cln_b, w_out, b_out):
    batch, seq, d = x.shape
    ctx_len = ctx.shape[1]
    depth = w_ada.shape[0]
    d_lru = conv4_w.shape[-1]
    alpha = (2 * depth) ** 0.25

    cond = jnp.concatenate([c, c_ctx[None, :], jnp.zeros((ADA_ROWS - batch - 1, d), F32)], axis=0)
    mods = _ada(cond, w_ada, b_ada).reshape(depth, ADA_ROWS, N_MOD, d)

    w_in_b, w_out_b = w_in.astype(BF16), w_out.astype(BF16)
    wr, wi = w_rg.astype(BF16), w_ig.astype(BF16)

    xs = x.reshape(batch * seq, d)
    cs = ctx.reshape(batch * ctx_len, d)
    zeros_h0 = jnp.zeros((batch, 1, d_lru), F32)
    for l in range(depth):
        last = l == depth - 1
        m = [mods[l, 0:batch, k][:, None, :] for k in range(N_MOD)]
        mc = [mods[l, batch:batch + 1, k][:, None, :] for k in range(N_MOD)]
        conv = (conv31_w[l], conv31_b[l], cln_g[l], cln_b[l])
        ln1, ln2, ln3 = ((ln_g[l, k], ln_b[l, k]) for k in range(3))

        def scan(xr_, h0_, direction, seq_len, **kw):
            return _scan(xr_, conv4_w[l], conv4_b[l], wr, b_rg[l, direction], wi, b_ig[l, direction],
                         lam[l, direction], h0_, layer=l, direction=direction, seq_len=seq_len, **kw)

        cs, ff1_b = _half_ffn(cs, mc[0], mc[1], mc[2], *ln1, alpha=alpha, w_f32=(ff1_in, ff1_out, l))
        if not last:
            xr_c, gl_c, yc_c = _mix_in(cs, mc[3], mc[4], w_in_b, *conv, layer=l, seg=ctx_len, d_lru=d_lru,
                                       conv_out=True)
        else:
            (xr_c,) = _mix_in(cs, mc[3], mc[4], w_in_b, *conv, layer=l, seg=ctx_len, d_lru=d_lru,
                              conv_out=False)
        hb_c = scan(xr_c, zeros_h0, 1, ctx_len)
        if not last:
            yr_c, hf_c = scan(xr_c, zeros_h0, 0, ctx_len, other=hb_c, gelu=gl_c)
            cs = _mix_out(cs, mc[5], yr_c, yc_c, w_out_b, b_out[l], *ln2, layer=l, alpha=alpha)
            cs, ff2_b = _half_ffn(cs, mc[6], mc[7], mc[8], *ln3, alpha=alpha, w_f32=(ff2_in, ff2_out, l))
        else:
            hf_c = scan(xr_c, zeros_h0, 0, ctx_len)
            ff2_b = _cast_ffn_weights(ff2_in, ff2_out, l)
        h0_b = hb_c.reshape(batch, ctx_len, d_lru)[:, 0:1, :]
        h0_f = hf_c.reshape(batch, ctx_len, d_lru)[:, ctx_len - 1:ctx_len, :]

        xs = _half_ffn(xs, m[0], m[1], m[2], *ln1, alpha=alpha, w_bf16=ff1_b)
        xr, gl, yc = _mix_in(xs, m[3], m[4], w_in_b, *conv, layer=l, seg=GRID_W, d_lru=d_lru, conv_out=True)
        hb = scan(xr, h0_b, 1, seq)
        yr = scan(xr, h0_f, 0, seq, other=hb, gelu=gl, want_states=False)
        xs = _mix_out(xs, m[5], yr, yc, w_out_b, b_out[l], *ln2, layer=l, alpha=alpha)
        xs = _half_ffn(xs, m[6], m[7], m[8], *ln3, alpha=alpha, w_bf16=ff2_b)
    return xs.reshape(batch, seq, d)
```

```python
import functools

import jax
import jax.numpy as jnp
from jax import lax
from jax.experimental import pallas as pl
from jax.experimental.pallas import tpu as pltpu

GRID_W = 64
N_LRU_HEADS = 4
RG_C = 8.0
CONV_SHORT = 4
SHORT_PAD_L = 2
CONV_K = 31
N_MOD = 9
ADA_ROWS = 8
EPS = 1e-6

SUBLANES = 8
LANES = 128
CONV_ROWS = 64
CONV_LANES = LANES
FFN_ROWS = 1024
FFN_COLS = 512
FFN_CAST_COLS = 256
FFN_TAIL_ROWS = 256
PROJ_COLS = 256
VMEM_LIMIT = 56 * 1024 * 1024

F32 = jnp.float32
BF16 = jnp.bfloat16


def _params(semantics):
    return pltpu.CompilerParams(dimension_semantics=semantics, vmem_limit_bytes=VMEM_LIMIT)


def _sigmoid(x):
    return 0.5 + 0.5 * jnp.tanh(0.5 * x)


def _layernorm(z, g, b):
    mu = jnp.mean(z, axis=-1, keepdims=True)
    d = z - mu
    var = jnp.mean(d * d, axis=-1, keepdims=True)
    return d * lax.rsqrt(var + EPS) * g + b


def _gelu_tanh(x):
    return 0.5 * x * (1.0 + jnp.tanh(0.7978845608028654 * (x + 0.044715 * (x * x * x))))


def _ada_kernel(c_ref, w_ref, b_ref, o_ref):
    c = c_ref[...]
    a = c * _sigmoid(c)
    o_ref[0] = jnp.dot(a, w_ref[0], preferred_element_type=F32) + b_ref[0]


def _ada(cond, w_ada, b_ada, tn=1024):
    depth, d, n = w_ada.shape
    return pl.pallas_call(
        _ada_kernel,
        grid=(depth, n // tn),
        in_specs=[
            pl.BlockSpec((ADA_ROWS, d), lambda l, j: (0, 0)),
            pl.BlockSpec((1, d, tn), lambda l, j: (l, 0, j)),
            pl.BlockSpec((1, 1, tn), lambda l, j: (l, 0, j)),
        ],
        out_specs=pl.BlockSpec((1, ADA_ROWS, tn), lambda l, j: (l, 0, j)),
        out_shape=jax.ShapeDtypeStruct((depth, ADA_ROWS, n), F32),
        compiler_params=_params(("parallel", "parallel")),
        name="ada",
    )(cond, w_ada, b_ada.reshape(depth, 1, n))


def _ffn_kernel(s_ref, sh_ref, sc_ref, gt_ref, w1g_ref, w1u_ref, w2_ref, g_ref, b_ref, o_ref, *rest,
                alpha, n_chunks, emit_weights):
    if emit_weights:
        w1g_out, w1u_out, w2_out, h_ref = rest
    else:
        (h_ref,) = rest
    j = pl.program_id(1)
    tm = s_ref.shape[0]
    assert n_chunks >= 2

    def weights():
        if not emit_weights:
            return w1g_ref[...], w1u_ref[...], w2_ref[...]
        ws = [ref[...].astype(BF16) for ref in (w1g_ref, w1u_ref, w2_ref)]
        for out_ref, w in zip((w1g_out, w1u_out, w2_out), ws):
            out_ref[...] = w
        return ws

    def chunk(first, last):
        w1g, w1u, w2 = weights()
        if first:
            h = (s_ref[...] * (1.0 + sc_ref[0]) + sh_ref[0]).astype(BF16)
            h_ref[...] = h
        else:
            h = h_ref[...]
        gate = jnp.dot(h, w1g, preferred_element_type=F32)
        up = jnp.dot(h, w1u, preferred_element_type=F32)
        act = (gate * _sigmoid(gate) * up).astype(BF16)
        if first:
            o_ref[...] = jnp.dot(act, w2, preferred_element_type=F32)
        elif not last:
            o_ref[...] += jnp.dot(act, w2, preferred_element_type=F32)
        else:
            for r0 in range(0, tm, FFN_TAIL_ROWS):
                rows = slice(r0, r0 + FFN_TAIL_ROWS)
                y = o_ref[rows, :] + jnp.dot(act[rows], w2, preferred_element_type=F32)
                z = alpha * s_ref[rows, :] + 0.5 * gt_ref[0] * y
                o_ref[rows, :] = _layernorm(z, g_ref[...], b_ref[...])

    pl.when(j == 0)(functools.partial(chunk, True, False))
    pl.when((j > 0) & (j < n_chunks - 1))(functools.partial(chunk, False, False))
    pl.when(j == n_chunks - 1)(functools.partial(chunk, False, True))


def _ffn_weight_specs(d, d_ff, tf, layer):
    n_chunks = d_ff // tf
    if layer is None:
        return [pl.BlockSpec((d, tf), lambda *ij: (0, ij[-1])),
                pl.BlockSpec((d, tf), lambda *ij: (0, ij[-1])),
                pl.BlockSpec((tf, d), lambda *ij: (ij[-1], 0))]
    return [pl.BlockSpec((None, d, tf), lambda *ij: (layer, 0, ij[-1])),
            pl.BlockSpec((None, d, tf), lambda *ij: (layer, 0, ij[-1] + n_chunks)),
            pl.BlockSpec((None, tf, d), lambda *ij: (layer, ij[-1], 0))]


def _ffn_weight_shapes(d, d_ff):
    return [jax.ShapeDtypeStruct((d, d_ff), BF16), jax.ShapeDtypeStruct((d, d_ff), BF16),
            jax.ShapeDtypeStruct((d_ff, d), BF16)]


def _half_ffn(s, shift, scale, gate, g, b, *, alpha, w_bf16=None, w_f32=None):
    n, d = s.shape
    emit = w_f32 is not None
    if emit:
        w1, w2, layer = w_f32
        d_ff, tf, w_args = w2.shape[1], FFN_CAST_COLS, (w1, w1, w2)
    else:
        layer, d_ff, tf, w_args = None, w_bf16[2].shape[0], FFN_COLS, w_bf16
    groups = shift.shape[0]
    tm = min(FFN_ROWS, n // groups)
    tiles_per_group = n // groups // tm
    n_chunks = d_ff // tf
    mod_spec = pl.BlockSpec((1, 1, d), lambda i, j: (i // tiles_per_group, 0, 0))
    vec_spec = pl.BlockSpec((1, d), lambda i, j: (0, 0))
    row_spec = pl.BlockSpec((tm, d), lambda i, j: (i, 0))
    out_specs, out_shape = [row_spec], [jax.ShapeDtypeStruct((n, d), F32)]
    if emit:
        assert n == tm, "each weight chunk must be emitted exactly once"
        out_specs += _ffn_weight_specs(d, d_ff, tf, None)
        out_shape += _ffn_weight_shapes(d, d_ff)
    out = pl.pallas_call(
        functools.partial(_ffn_kernel, alpha=alpha, n_chunks=n_chunks, emit_weights=emit),
        grid=(n // tm, n_chunks),
        in_specs=[row_spec, mod_spec, mod_spec, mod_spec] + _ffn_weight_specs(d, d_ff, tf, layer)
        + [vec_spec, vec_spec],
        out_specs=out_specs,
        out_shape=out_shape,
        scratch_shapes=[pltpu.VMEM((tm, d), BF16)],
        compiler_params=_params(("parallel", "arbitrary")),
        name="half_ffn",
    )(s, shift, scale, gate, *w_args, g.reshape(1, d), b.reshape(1, d))
    return (out[0], tuple(out[1:])) if emit else out[0]


def _cast_kernel(w1g_ref, w1u_ref, w2_ref, w1g_out, w1u_out, w2_out):
    for src_ref, dst_ref in ((w1g_ref, w1g_out), (w1u_ref, w1u_out), (w2_ref, w2_out)):
        dst_ref[...] = src_ref[...].astype(BF16)


def _cast_ffn_weights(w1, w2, layer, tf=512):
    d, d_ff = w1.shape[1], w2.shape[1]
    return tuple(pl.pallas_call(
        _cast_kernel,
        grid=(d_ff // tf,),
        in_specs=_ffn_weight_specs(d, d_ff, tf, layer),
        out_specs=_ffn_weight_specs(d, d_ff, tf, None),
        out_shape=_ffn_weight_shapes(d, d_ff),
        compiler_params=_params(("parallel",)),
        name="cast_ffn_weights",
    )(w1, w1, w2))


def _mix_in_kernel(x_ref, sh_ref, sc_ref, w_ref, *rest, seg, conv_out):
    if conv_out:
        c31w_ref, c31b_ref, cg_ref, cb_ref, xr_ref, gl_ref, yc_ref, upad_ref, conv_ref = rest
    else:
        (xr_ref,) = rest
    d_lru = xr_ref.shape[1]
    h = (x_ref[...] * (1.0 + sc_ref[0]) + sh_ref[0]).astype(BF16)
    if not conv_out:
        xr_ref[...] = jnp.dot(h, w_ref[...], preferred_element_type=F32)
        return
    d_conv = yc_ref.shape[1]
    tm = x_ref.shape[0]
    nseg = tm // seg
    half = CONV_K // 2
    lead = 2 * SUBLANES
    tail = upad_ref.shape[1] - lead - seg

    upad_ref[:, 0:lead, :] = jnp.zeros((nseg, lead, d_conv), F32)
    upad_ref[:, lead + seg:, :] = jnp.zeros((nseg, tail, d_conv), F32)

    def proj(c0):
        return jnp.dot(h, w_ref[:, c0:c0 + PROJ_COLS], preferred_element_type=F32)

    assert d_conv == d_lru
    for c0 in range(0, d_lru, PROJ_COLS):
        u = proj(2 * d_lru + c0) * _sigmoid(proj(2 * d_lru + d_conv + c0))
        upad_ref[:, lead:lead + seg, c0:c0 + PROJ_COLS] = u.reshape(nseg, seg, PROJ_COLS)
        xr_ref[:, c0:c0 + PROJ_COLS] = proj(c0)
        gl_ref[:, c0:c0 + PROJ_COLS] = _gelu_tanh(proj(d_lru + c0))

    offs = [lead - half + k for k in range(CONV_K)]
    slab_rows = CONV_ROWS + lead + tail
    for s in range(nseg):
        for rb in range(seg // CONV_ROWS):
            for cc in range(d_conv // CONV_LANES):
                cs = slice(cc * CONV_LANES, (cc + 1) * CONV_LANES)
                slab = upad_ref[s, rb * CONV_ROWS:rb * CONV_ROWS + slab_rows, cs]
                acc = jnp.broadcast_to(c31b_ref[:, cs], (CONV_ROWS, CONV_LANES))
                for r in range(SUBLANES):
                    rot = slab if r == 0 else pltpu.roll(slab, slab_rows - r, 0)
                    for k in range(CONV_K):
                        if offs[k] % SUBLANES == r:
                            q = offs[k] - r
                            assert q + CONV_ROWS <= slab_rows - r
                            acc = acc + c31w_ref[k:k + 1, cs] * rot[q:q + CONV_ROWS]
                row0 = s * seg + rb * CONV_ROWS
                conv_ref[row0:row0 + CONV_ROWS, cs] = acc
    y = _layernorm(conv_ref[...], cg_ref[...], cb_ref[...])
    yc_ref[...] = (y * _sigmoid(y)).astype(BF16)


def _mix_in(x, shift, scale, w_in, c31w, c31b, cg, cb, *, layer, seg, d_lru, conv_out, tm=512):
    n, d = x.shape
    d_in = w_in.shape[2]
    d_conv = (d_in - 2 * d_lru) // 2
    groups = shift.shape[0]
    tiles_per_group = n // groups // tm
    mod_spec = pl.BlockSpec((1, 1, d), lambda i: (i // tiles_per_group, 0, 0))
    row_lru = pl.BlockSpec((tm, d_lru), lambda i: (i, 0))
    in_specs = [pl.BlockSpec((tm, d), lambda i: (i, 0)), mod_spec, mod_spec]
    args = [x, shift, scale]
    if conv_out:
        in_specs += [
            pl.BlockSpec((None, d, d_in), lambda i: (layer, 0, 0)),
            pl.BlockSpec((CONV_K, d_conv), lambda i: (0, 0)),
            pl.BlockSpec((1, d_conv), lambda i: (0, 0)),
            pl.BlockSpec((1, d_conv), lambda i: (0, 0)),
            pl.BlockSpec((1, d_conv), lambda i: (0, 0)),
        ]
        args += [w_in, c31w, c31b.reshape(1, d_conv), cg.reshape(1, d_conv), cb.reshape(1, d_conv)]
        out_specs = [row_lru, row_lru, pl.BlockSpec((tm, d_conv), lambda i: (i, 0))]
        out_shape = [jax.ShapeDtypeStruct((n, d_lru), F32), jax.ShapeDtypeStruct((n, d_lru), F32),
                     jax.ShapeDtypeStruct((n, d_conv), BF16)]
        pad_rows = 2 * SUBLANES + seg + 2 * SUBLANES
        scratch = [pltpu.VMEM((tm // seg, pad_rows, d_conv), F32), pltpu.VMEM((tm, d_conv), F32)]
    else:
        in_specs += [pl.BlockSpec((None, d, d_lru), lambda i: (layer, 0, 0))]
        args += [w_in]
        out_specs = [row_lru]
        out_shape = [jax.ShapeDtypeStruct((n, d_lru), F32)]
        scratch = []
    return pl.pallas_call(
        functools.partial(_mix_in_kernel, seg=seg, conv_out=conv_out),
        grid=(n // tm,),
        in_specs=in_specs,
        out_specs=out_specs,
        out_shape=out_shape,
        scratch_shapes=scratch,
        compiler_params=_params(("parallel",)),
        name="mix_in",
    )(*args)


def _scan_kernel(xr_ref, xp_ref, xn_ref, c4w_ref, c4b_ref, wr_ref, br_ref, wi_ref, bi_ref,
                 lam_ref, h0_ref, *rest, reverse, tiles_per_seq, n_tiles, fuse_out):
    if fuse_out:
        hb_ref, gl_ref, o_ref, hs_ref, ext_ref, a_ref, b_ref, carry_ref = rest
    else:
        hs_ref, ext_ref, a_ref, b_ref, carry_ref = rest
    i = pl.program_id(0)
    t = (n_tiles - 1 - i) if reverse else i
    p = t % tiles_per_seq
    first = p == 0
    last = p == tiles_per_seq - 1
    tm, c = xr_ref.shape
    hd = c // N_LRU_HEADS

    ext_ref[0:SUBLANES, :] = jnp.where(first, 0.0, xp_ref[...])
    ext_ref[SUBLANES:SUBLANES + tm, :] = xr_ref[...]
    ext_ref[SUBLANES + tm:, :] = jnp.where(last, 0.0, xn_ref[...])
    ext = ext_ref[...]
    xc = jnp.zeros((tm, c), F32) + c4b_ref[...]
    for k in range(CONV_SHORT):
        off = SUBLANES - SHORT_PAD_L + k
        shifted = ext if off % SUBLANES == 0 else pltpu.roll(ext, ext.shape[0] - off % SUBLANES, 0)
        base = off - off % SUBLANES
        xc = xc + c4w_ref[k:k + 1, :] * shifted[base:base + tm, :]

    xcb = xc.astype(BF16)
    lam = lam_ref[...]
    softplus_neg_lam = jnp.maximum(-lam, 0.0) + jnp.log(1.0 + jnp.exp(-jnp.abs(lam)))
    coef = -RG_C * softplus_neg_lam
    for hh in range(N_LRU_HEADS):
        cs = slice(hh * hd, (hh + 1) * hd)
        xh = xcb[:, cs]
        r = _sigmoid(jnp.dot(xh, wr_ref[hh], preferred_element_type=F32) + br_ref[:, cs])
        ig = _sigmoid(jnp.dot(xh, wi_ref[hh], preferred_element_type=F32) + bi_ref[:, cs])
        log_a = coef[:, cs] * r
        a = jnp.exp(log_a)
        a_ref[:, cs] = a
        b_ref[:, cs] = jnp.sqrt(1.0 - a * a) * (ig * xc[:, cs])

    start = last if reverse else first

    @pl.when(start)
    def _():
        carry_ref[...] = h0_ref[0]

    n_groups = tm // SUBLANES
    row = lax.broadcasted_iota(jnp.int32, (SUBLANES, c), 0)

    def group(k, carry):
        kk = (n_groups - 1 - k) if reverse else k
        r0 = pl.multiple_of(kk * SUBLANES, SUBLANES)
        a = a_ref[pl.ds(r0, SUBLANES), :]
        b = b_ref[pl.ds(r0, SUBLANES), :]
        for s in (1, 2, 4):
            if reverse:
                shift, keep = SUBLANES - s, row < SUBLANES - s
            else:
                shift, keep = s, row >= s
            a_prev = jnp.where(keep, pltpu.roll(a, shift, 0), 1.0)
            b_prev = jnp.where(keep, pltpu.roll(b, shift, 0), 0.0)
            b = b + a * b_prev
            a = a * a_prev
        h = b + a * carry
        hs_ref[pl.ds(r0, SUBLANES), :] = h
        return h[0:1, :] if reverse else h[SUBLANES - 1:SUBLANES, :]

    carry_ref[...] = lax.fori_loop(0, n_groups, group, carry_ref[...])

    if fuse_out:
        o_ref[...] = ((hs_ref[...] + hb_ref[...]) * gl_ref[...]).astype(BF16)


def _scan(xr, c4w, c4b, wr, br, wi, bi, lam, h0, *, layer, direction, seq_len, other=None, gelu=None,
          want_states=True, tm=512):
    reverse = direction == 1
    n, c = xr.shape
    tm = min(tm, seq_len)
    n_tiles = n // tm
    tiles_per_seq = seq_len // tm
    hd = c // N_LRU_HEADS
    fuse_out = other is not None
    blocks_per_tile = tm // SUBLANES
    n_blocks = n // SUBLANES

    def tile(i):
        return (n_tiles - 1 - i) if reverse else i

    row_spec = pl.BlockSpec((tm, c), lambda i: (tile(i), 0))
    vec_spec = pl.BlockSpec((1, c), lambda i: (0, 0))
    w_spec = pl.BlockSpec((None, None, N_LRU_HEADS, hd, hd), lambda i: (layer, direction, 0, 0, 0))
    in_specs = [
        row_spec,
        pl.BlockSpec((SUBLANES, c), lambda i: (jnp.maximum(tile(i) * blocks_per_tile - 1, 0), 0)),
        pl.BlockSpec((SUBLANES, c), lambda i: (jnp.minimum((tile(i) + 1) * blocks_per_tile, n_blocks - 1), 0)),
        pl.BlockSpec((CONV_SHORT, c), lambda i: (0, 0)),
        vec_spec, w_spec, vec_spec, w_spec, vec_spec, vec_spec,
        pl.BlockSpec((1, 1, c), lambda i: (tile(i) // tiles_per_seq, 0, 0)),
    ]
    args = [xr, xr, xr, c4w, c4b.reshape(1, c), wr, br.reshape(1, c), wi, bi.reshape(1, c),
            lam.reshape(1, c), h0]
    scratch = [pltpu.VMEM((tm + 2 * SUBLANES, c), F32), pltpu.VMEM((tm, c), F32),
               pltpu.VMEM((tm, c), F32), pltpu.VMEM((1, c), F32)]
    states = jax.ShapeDtypeStruct((n, c), F32)
    fused = jax.ShapeDtypeStruct((n, c), BF16)
    if fuse_out:
        in_specs += [row_spec, row_spec]
        args += [other, gelu]
    if fuse_out and want_states:
        out_specs, out_shape = [row_spec, row_spec], [fused, states]
    elif fuse_out:
        out_specs, out_shape = row_spec, fused
        scratch = [pltpu.VMEM((tm, c), F32)] + scratch
    else:
        out_specs, out_shape = row_spec, states
    return pl.pallas_call(
        functools.partial(_scan_kernel, reverse=reverse, tiles_per_seq=tiles_per_seq,
                          n_tiles=n_tiles, fuse_out=fuse_out),
        grid=(n_tiles,),
        in_specs=in_specs,
        out_specs=out_specs,
        out_shape=out_shape,
        scratch_shapes=scratch,
        compiler_params=_params(("arbitrary",)),
        name="lru_scan",
    )(*args)


def _mix_out_kernel(x_ref, gt_ref, yr_ref, yc_ref, wa_ref, wb_ref, bo_ref, g_ref, b_ref, o_ref, *, alpha):
    y = jnp.dot(yr_ref[...], wa_ref[...], preferred_element_type=F32)
    y = y + jnp.dot(yc_ref[...], wb_ref[...], preferred_element_type=F32) + bo_ref[...]
    z = alpha * x_ref[...] + gt_ref[0] * y
    o_ref[...] = _layernorm(z, g_ref[...], b_ref[...])


def _mix_out(x, gate, y_rec, y_conv, w_out, b_out, g, b, *, layer, alpha, tm=512):
    n, d = x.shape
    d_lru = y_rec.shape[1]
    d_conv = y_conv.shape[1]
    groups = gate.shape[0]
    tiles_per_group = n // groups // tm
    vec_spec = pl.BlockSpec((1, d), lambda i: (0, 0))
    assert d_lru == d_conv
    return pl.pallas_call(
        functools.partial(_mix_out_kernel, alpha=alpha),
        grid=(n // tm,),
        in_specs=[
            pl.BlockSpec((tm, d), lambda i: (i, 0)),
            pl.BlockSpec((1, 1, d), lambda i: (i // tiles_per_group, 0, 0)),
            pl.BlockSpec((tm, d_lru), lambda i: (i, 0)),
            pl.BlockSpec((tm, d_conv), lambda i: (i, 0)),
            pl.BlockSpec((None, d_lru, d), lambda i: (layer, 0, 0)),
            pl.BlockSpec((None, d_conv, d), lambda i: (layer, 1, 0)),
            vec_spec, vec_spec, vec_spec,
        ],
        out_specs=pl.BlockSpec((tm, d), lambda i: (i, 0)),
        out_shape=jax.ShapeDtypeStruct((n, d), F32),
        compiler_params=_params(("parallel",)),
        name="mix_out",
    )(x, gate, y_rec, y_conv, w_out, w_out, b_out.reshape(1, d), g.reshape(1, d), b.reshape(1, d))


def kernel(x, c, ctx, c_ctx, w_ada, b_ada, ln_g, ln_b, ff1_in, ff1_out, ff2_in, ff2_out, w_in, conv4_w,
           conv4_b, w_rg, b_rg, w_ig, b_ig, lam, conv31_w, conv31_b, cln_g, cln_b, w_out, b_out):
    batch, seq, d = x.shape
    ctx_len = ctx.shape[1]
    depth = w_ada.shape[0]
    d_lru = conv4_w.shape[-1]
    alpha = (2 * depth) ** 0.25

    cond = jnp.concatenate([c, c_ctx[None, :], jnp.zeros((ADA_ROWS - batch - 1, d), F32)], axis=0)
    mods = _ada(cond, w_ada, b_ada).reshape(depth, ADA_ROWS, N_MOD, d)

    w_in_b, w_out_b = w_in.astype(BF16), w_out.astype(BF16)
    wr, wi = w_rg.astype(BF16), w_ig.astype(BF16)

    xs = x.reshape(batch * seq, d)
    cs = ctx.reshape(batch * ctx_len, d)
    zeros_h0 = jnp.zeros((batch, 1, d_lru), F32)
    for l in range(depth):
        last = l == depth - 1
        m = [mods[l, 0:batch, k][:, None, :] for k in range(N_MOD)]
        mc = [mods[l, batch:batch + 1, k][:, None, :] for k in range(N_MOD)]
        conv = (conv31_w[l], conv31_b[l], cln_g[l], cln_b[l])
        ln1, ln2, ln3 = ((ln_g[l, k], ln_b[l, k]) for k in range(3))

        def scan(xr_, h0_, direction, seq_len, **kw):
            return _scan(xr_, conv4_w[l], conv4_b[l], wr, b_rg[l, direction], wi, b_ig[l, direction],
                         lam[l, direction], h0_, layer=l, direction=direction, seq_len=seq_len, **kw)

        cs, ff1_b = _half_ffn(cs, mc[0], mc[1], mc[2], *ln1, alpha=alpha, w_f32=(ff1_in, ff1_out, l))
        if not last:
            xr_c, gl_c, yc_c = _mix_in(cs, mc[3], mc[4], w_in_b, *conv, layer=l, seg=ctx_len, d_lru=d_lru,
                                       conv_out=True)
        else:
            (xr_c,) = _mix_in(cs, mc[3], mc[4], w_in_b, *conv, layer=l, seg=ctx_len, d_lru=d_lru,
                              conv_out=False)
        hb_c = scan(xr_c, zeros_h0, 1, ctx_len)
        if not last:
            yr_c, hf_c = scan(xr_c, zeros_h0, 0, ctx_len, other=hb_c, gelu=gl_c)
            cs = _mix_out(cs, mc[5], yr_c, yc_c, w_out_b, b_out[l], *ln2, layer=l, alpha=alpha)
            cs, ff2_b = _half_ffn(cs, mc[6], mc[7], mc[8], *ln3, alpha=alpha, w_f32=(ff2_in, ff2_out, l))
        else:
            hf_c = scan(xr_c, zeros_h0, 0, ctx_len)
            ff2_b = _cast_ffn_weights(ff2_in, ff2_out, l)
        h0_b = hb_c.reshape(batch, ctx_len, d_lru)[:, 0:1, :]
        h0_f = hf_c.reshape(batch, ctx_len, d_lru)[:, ctx_len - 1:ctx_len, :]

        xs = _half_ffn(xs, m[0], m[1], m[2], *ln1, alpha=alpha, w_bf16=ff1_b)
        xr, gl, yc = _mix_in(xs, m[3], m[4], w_in_b, *conv, layer=l, seg=GRID_W, d_lru=d_lru, conv_out=True)
        hb = scan(xr, h0_b, 1, seq)
        yr = scan(xr, h0_f, 0, seq, other=hb, gelu=gl, want_states=False)
        xs = _mix_out(xs, m[5], yr, yc, w_out_b, b_out[l], *ln2, layer=l, alpha=alpha)
        xs = _half_ffn(xs, m[6], m[7], m[8], *ln3, alpha=alpha, w_bf16=ff2_b)
    return xs.reshape(batch, seq, d)
```

```python
import functools

import jax
import jax.numpy as jnp
from jax import lax
from jax.experimental import pallas as pl
from jax.experimental.pallas import tpu as pltpu

GRID_W = 64
N_LRU_HEADS = 4
RG_C = 8.0
CONV_SHORT = 4
SHORT_PAD_L = 2
CONV_K = 31
N_MOD = 9
ADA_ROWS = 8
EPS = 1e-6

SUBLANES = 8
LANES = 128
CONV_ROWS = 64
CONV_LANES = LANES
FFN_ROWS = 1024
FFN_COLS = 512
FFN_CAST_COLS = 256
FFN_TAIL_ROWS = 256
PROJ_COLS = 256
VMEM_LIMIT = 56 * 1024 * 1024

F32 = jnp.float32
BF16 = jnp.bfloat16


def _params(semantics):
    return pltpu.CompilerParams(dimension_semantics=semantics, vmem_limit_bytes=VMEM_LIMIT)


def _sigmoid(x):
    return 0.5 + 0.5 * jnp.tanh(0.5 * x)


def _layernorm(z, g, b):
    mu = jnp.mean(z, axis=-1, keepdims=True)
    d = z - mu
    var = jnp.mean(d * d, axis=-1, keepdims=True)
    return d * lax.rsqrt(var + EPS) * g + b


def _gelu_tanh(x):
    return 0.5 * x * (1.0 + jnp.tanh(0.7978845608028654 * (x + 0.044715 * (x * x * x))))


def _ada_kernel(c_ref, w_ref, b_ref, o_ref):
    c = c_ref[...]
    a = c * _sigmoid(c)
    o_ref[0] = jnp.dot(a, w_ref[0], preferred_element_type=F32) + b_ref[0]


def _ada(cond, w_ada, b_ada, tn=1024):
    depth, d, n = w_ada.shape
    return pl.pallas_call(
        _ada_kernel,
        grid=(depth, n // tn),
        in_specs=[
            pl.BlockSpec((ADA_ROWS, d), lambda l, j: (0, 0)),
            pl.BlockSpec((1, d, tn), lambda l, j: (l, 0, j)),
            pl.BlockSpec((1, 1, tn), lambda l, j: (l, 0, j)),
        ],
        out_specs=pl.BlockSpec((1, ADA_ROWS, tn), lambda l, j: (l, 0, j)),
        out_shape=jax.ShapeDtypeStruct((depth, ADA_ROWS, n), F32),
        compiler_params=_params(("parallel", "parallel")),
        name="ada",
    )(cond, w_ada, b_ada.reshape(depth, 1, n))


def _ffn_kernel(s_ref, sh_ref, sc_ref, gt_ref, w1g_ref, w1u_ref, w2_ref, g_ref, b_ref, o_ref, *rest,
                alpha, n_chunks, emit_weights):
    if emit_weights:
        w1g_out, w1u_out, w2_out, h_ref = rest
    else:
        (h_ref,) = rest
    j = pl.program_id(1)
    tm = s_ref.shape[0]
    assert n_chunks >= 2

    def weights():
        if not emit_weights:
            return w1g_ref[...], w1u_ref[...], w2_ref[...]
        ws = [ref[...].astype(BF16) for ref in (w1g_ref, w1u_ref, w2_ref)]
        for out_ref, w in zip((w1g_out, w1u_out, w2_out), ws):
            out_ref[...] = w
        return ws

    def chunk(first, last):
        w1g, w1u, w2 = weights()
        if first:
            h = (s_ref[...] * (1.0 + sc_ref[0]) + sh_ref[0]).astype(BF16)
            h_ref[...] = h
        else:
            h = h_ref[...]
        gate = jnp.dot(h, w1g, preferred_element_type=F32)
        up = jnp.dot(h, w1u, preferred_element_type=F32)
        act = (gate * _sigmoid(gate) * up).astype(BF16)
        if first:
            o_ref[...] = jnp.dot(act, w2, preferred_element_type=F32)
        elif not last:
            o_ref[...] += jnp.dot(act, w2, preferred_element_type=F32)
        else:
            for r0 in range(0, tm, FFN_TAIL_ROWS):
                rows = slice(r0, r0 + FFN_TAIL_ROWS)
                y = o_ref[rows, :] + jnp.dot(act[rows], w2, preferred_element_type=F32)
                z = alpha * s_ref[rows, :] + 0.5 * gt_ref[0] * y
                o_ref[rows, :] = _layernorm(z, g_ref[...], b_ref[...])

    pl.when(j == 0)(functools.partial(chunk, True, False))
    pl.when((j > 0) & (j < n_chunks - 1))(functools.partial(chunk, False, False))
    pl.when(j == n_chunks - 1)(functools.partial(chunk, False, True))


def _ffn_weight_specs(d, d_ff, tf, layer):
    n_chunks = d_ff // tf
    if layer is None:
        return [pl.BlockSpec((d, tf), lambda *ij: (0, ij[-1])),
                pl.BlockSpec((d, tf), lambda *ij: (0, ij[-1])),
                pl.BlockSpec((tf, d), lambda *ij: (ij[-1], 0))]
    return [pl.BlockSpec((None, d, tf), lambda *ij: (layer, 0, ij[-1])),
            pl.BlockSpec((None, d, tf), lambda *ij: (layer, 0, ij[-1] + n_chunks)),
            pl.BlockSpec((None, tf, d), lambda *ij: (layer, ij[-1], 0))]


def _ffn_weight_shapes(d, d_ff):
    return [jax.ShapeDtypeStruct((d, d_ff), BF16), jax.ShapeDtypeStruct((d, d_ff), BF16),
            jax.ShapeDtypeStruct((d_ff, d), BF16)]


def _half_ffn(s, shift, scale, gate, g, b, *, alpha, w_bf16=None, w_f32=None):
    n, d = s.shape
    emit = w_f32 is not None
    if emit:
        w1, w2, layer = w_f32
        d_ff, tf, w_args = w2.shape[1], FFN_CAST_COLS, (w1, w1, w2)
    else:
        layer, d_ff, tf, w_args = None, w_bf16[2].shape[0], FFN_COLS, w_bf16
    groups = shift.shape[0]
    tm = min(FFN_ROWS, n // groups)
    tiles_per_group = n // groups // tm
    n_chunks = d_ff // tf
    mod_spec = pl.BlockSpec((1, 1, d), lambda i, j: (i // tiles_per_group, 0, 0))
    vec_spec = pl.BlockSpec((1, d), lambda i, j: (0, 0))
    row_spec = pl.BlockSpec((tm, d), lambda i, j: (i, 0))
    out_specs, out_shape = [row_spec], [jax.ShapeDtypeStruct((n, d), F32)]
    if emit:
        assert n == tm, "each weight chunk must be emitted exactly once"
        out_specs += _ffn_weight_specs(d, d_ff, tf, None)
        out_shape += _ffn_weight_shapes(d, d_ff)
    out = pl.pallas_call(
        functools.partial(_ffn_kernel, alpha=alpha, n_chunks=n_chunks, emit_weights=emit),
        grid=(n // tm, n_chunks),
        in_specs=[row_spec, mod_spec, mod_spec, mod_spec] + _ffn_weight_specs(d, d_ff, tf, layer)
        + [vec_spec, vec_spec],
        out_specs=out_specs,
        out_shape=out_shape,
        scratch_shapes=[pltpu.VMEM((tm, d), BF16)],
        compiler_params=_params(("parallel", "arbitrary")),
        name="half_ffn",
    )(s, shift, scale, gate, *w_args, g.reshape(1, d), b.reshape(1, d))
    return (out[0], tuple(out[1:])) if emit else out[0]


def _cast_kernel(w1g_ref, w1u_ref, w2_ref, w1g_out, w1u_out, w2_out):
    for src_ref, dst_ref in ((w1g_ref, w1g_out), (w1u_ref, w1u_out), (w2_ref, w2_out)):
        dst_ref[...] = src_ref[...].astype(BF16)


def _cast_ffn_weights(w1, w2, layer, tf=512):
    d, d_ff = w1.shape[1], w2.shape[1]
    return tuple(pl.pallas_call(
        _cast_kernel,
        grid=(d_ff // tf,),
        in_specs=_ffn_weight_specs(d, d_ff, tf, layer),
        out_specs=_ffn_weight_specs(d, d_ff, tf, None),
        out_shape=_ffn_weight_shapes(d, d_ff),
        compiler_params=_params(("parallel",)),
        name="cast_ffn_weights",
    )(w1, w1, w2))


def _mix_in_kernel(x_ref, sh_ref, sc_ref, w_ref, *rest, seg, conv_out):
    if conv_out:
        c31w_ref, c31b_ref, cg_ref, cb_ref, xr_ref, gl_ref, yc_ref, upad_ref, conv_ref = rest
    else:
        (xr_ref,) = rest
    d_lru = xr_ref.shape[1]
    h = (x_ref[...] * (1.0 + sc_ref[0]) + sh_ref[0]).astype(BF16)
    if not conv_out:
        xr_ref[...] = jnp.dot(h, w_ref[...], preferred_element_type=F32)
        return
    d_conv = yc_ref.shape[1]
    tm = x_ref.shape[0]
    nseg = tm // seg
    half = CONV_K // 2
    lead = 2 * SUBLANES
    tail = upad_ref.shape[1] - lead - seg

    upad_ref[:, 0:lead, :] = jnp.zeros((nseg, lead, d_conv), F32)
    upad_ref[:, lead + seg:, :] = jnp.zeros((nseg, tail, d_conv), F32)

    def proj(c0):
        return jnp.dot(h, w_ref[:, c0:c0 + PROJ_COLS], preferred_element_type=F32)

    assert d_conv == d_lru
    for c0 in range(0, d_lru, PROJ_COLS):
        u = proj(2 * d_lru + c0) * _sigmoid(proj(2 * d_lru + d_conv + c0))
        upad_ref[:, lead:lead + seg, c0:c0 + PROJ_COLS] = u.reshape(nseg, seg, PROJ_COLS)
        xr_ref[:, c0:c0 + PROJ_COLS] = proj(c0)
        gl_ref[:, c0:c0 + PROJ_COLS] = _gelu_tanh(proj(d_lru + c0))

    offs = [lead - half + k for k in range(CONV_K)]
    slab_rows = CONV_ROWS + lead + tail
    for s in range(nseg):
        for rb in range(seg // CONV_ROWS):
            for cc in range(d_conv // CONV_LANES):
                cs = slice(cc * CONV_LANES, (cc + 1) * CONV_LANES)
                slab = upad_ref[s, rb * CONV_ROWS:rb * CONV_ROWS + slab_rows, cs]
                acc = jnp.broadcast_to(c31b_ref[:, cs], (CONV_ROWS, CONV_LANES))
                for r in range(SUBLANES):
                    rot = slab if r == 0 else pltpu.roll(slab, slab_rows - r, 0)
                    for k in range(CONV_K):
                        if offs[k] % SUBLANES == r:
                            q = offs[k] - r
                            assert q + CONV_ROWS <= slab_rows - r
                            acc = acc + c31w_ref[k:k + 1, cs] * rot[q:q + CONV_ROWS]
                row0 = s * seg + rb * CONV_ROWS
                conv_ref[row0:row0 + CONV_ROWS, cs] = acc
    y = _layernorm(conv_ref[...], cg_ref[...], cb_ref[...])
    yc_ref[...] = (y * _sigmoid(y)).astype(BF16)


def _mix_in(x, shift, scale, w_in, c31w, c31b, cg, cb, *, layer, seg, d_lru, conv_out, tm=512):
    n, d = x.shape
    d_in = w_in.shape[2]
    d_conv = (d_in - 2 * d_lru) // 2
    groups = shift.shape[0]
    tiles_per_group = n // groups // tm
    mod_spec = pl.BlockSpec((1, 1, d), lambda i: (i // tiles_per_group, 0, 0))
    row_lru = pl.BlockSpec((tm, d_lru), lambda i: (i, 0))
    in_specs = [pl.BlockSpec((tm, d), lambda i: (i, 0)), mod_spec, mod_spec]
    args = [x, shift, scale]
    if conv_out:
        in_specs += [
            pl.BlockSpec((None, d, d_in), lambda i: (layer, 0, 0)),
            pl.BlockSpec((CONV_K, d_conv), lambda i: (0, 0)),
            pl.BlockSpec((1, d_conv), lambda i: (0, 0)),
            pl.BlockSpec((1, d_conv), lambda i: (0, 0)),
            pl.BlockSpec((1, d_conv), lambda i: (0, 0)),
        ]
        args += [w_in, c31w, c31b.reshape(1, d_conv), cg.reshape(1, d_conv), cb.reshape(1, d_conv)]
        out_specs = [row_lru, row_lru, pl.BlockSpec((tm, d_conv), lambda i: (i, 0))]
        out_shape = [jax.ShapeDtypeStruct((n, d_lru), F32), jax.ShapeDtypeStruct((n, d_lru), F32),
                     jax.ShapeDtypeStruct((n, d_conv), BF16)]
        pad_rows = 2 * SUBLANES + seg + 2 * SUBLANES
        scratch = [pltpu.VMEM((tm // seg, pad_rows, d_conv), F32), pltpu.VMEM((tm, d_conv), F32)]
    else:
        in_specs += [pl.BlockSpec((None, d, d_lru), lambda i: (layer, 0, 0))]
        args += [w_in]
        out_specs = [row_lru]
        out_shape = [jax.ShapeDtypeStruct((n, d_lru), F32)]
        scratch = []
    return pl.pallas_call(
        functools.partial(_mix_in_kernel, seg=seg, conv_out=conv_out),
        grid=(n // tm,),
        in_specs=in_specs,
        out_specs=out_specs,
        out_shape=out_shape,
        scratch_shapes=scratch,
        compiler_params=_params(("parallel",)),
        name="mix_in",
    )(*args)


def _scan_kernel(xr_ref, xp_ref, xn_ref, c4w_ref, c4b_ref, wr_ref, br_ref, wi_ref, bi_ref,
                 lam_ref, h0_ref, *rest, reverse, tiles_per_seq, n_tiles, fuse_out):
    if fuse_out:
        hb_ref, gl_ref, o_ref, hs_ref, ext_ref, a_ref, b_ref, carry_ref = rest
    else:
        hs_ref, ext_ref, a_ref, b_ref, carry_ref = rest
    i = pl.program_id(0)
    t = (n_tiles - 1 - i) if reverse else i
    p = t % tiles_per_seq
    first = p == 0
    last = p == tiles_per_seq - 1
    tm, c = xr_ref.shape
    hd = c // N_LRU_HEADS

    ext_ref[0:SUBLANES, :] = jnp.where(first, 0.0, xp_ref[...])
    ext_ref[SUBLANES:SUBLANES + tm, :] = xr_ref[...]
    ext_ref[SUBLANES + tm:, :] = jnp.where(last, 0.0, xn_ref[...])
    ext = ext_ref[...]
    xc = jnp.zeros((tm, c), F32) + c4b_ref[...]
    for k in range(CONV_SHORT):
        off = SUBLANES - SHORT_PAD_L + k
        shifted = ext if off % SUBLANES == 0 else pltpu.roll(ext, ext.shape[0] - off % SUBLANES, 0)
        base = off - off % SUBLANES
        xc = xc + c4w_ref[k:k + 1, :] * shifted[base:base + tm, :]

    xcb = xc.astype(BF16)
    lam = lam_ref[...]
    softplus_neg_lam = jnp.maximum(-lam, 0.0) + jnp.log(1.0 + jnp.exp(-jnp.abs(lam)))
    coef = -RG_C * softplus_neg_lam
    for hh in range(N_LRU_HEADS):
        cs = slice(hh * hd, (hh + 1) * hd)
        xh = xcb[:, cs]
        r = _sigmoid(jnp.dot(xh, wr_ref[hh], preferred_element_type=F32) + br_ref[:, cs])
        ig = _sigmoid(jnp.dot(xh, wi_ref[hh], preferred_element_type=F32) + bi_ref[:, cs])
        log_a = coef[:, cs] * r
        a = jnp.exp(log_a)
        a_ref[:, cs] = a
        b_ref[:, cs] = jnp.sqrt(1.0 - a * a) * (ig * xc[:, cs])

    start = last if reverse else first

    @pl.when(start)
    def _():
        carry_ref[...] = h0_ref[0]

    n_groups = tm // SUBLANES
    row = lax.broadcasted_iota(jnp.int32, (SUBLANES, c), 0)

    def group(k, carry):
        kk = (n_groups - 1 - k) if reverse else k
        r0 = pl.multiple_of(kk * SUBLANES, SUBLANES)
        a = a_ref[pl.ds(r0, SUBLANES), :]
        b = b_ref[pl.ds(r0, SUBLANES), :]
        for s in (1, 2, 4):
            if reverse:
                shift, keep = SUBLANES - s, row < SUBLANES - s
            else:
                shift, keep = s, row >= s
            a_prev = jnp.where(keep, pltpu.roll(a, shift, 0), 1.0)
            b_prev = jnp.where(keep, pltpu.roll(b, shift, 0), 0.0)
            b = b + a * b_prev
            a = a * a_prev
        h = b + a * carry
        hs_ref[pl.ds(r0, SUBLANES), :] = h
        return h[0:1, :] if reverse else h[SUBLANES - 1:SUBLANES, :]

    carry_ref[...] = lax.fori_loop(0, n_groups, group, carry_ref[...], unroll=2)

    if fuse_out:
        o_ref[...] = ((hs_ref[...] + hb_ref[...]) * gl_ref[...]).astype(BF16)


def _scan(xr, c4w, c4b, wr, br, wi, bi, lam, h0, *, layer, direction, seq_len, other=None, gelu=None,
          want_states=True, tm=512):
    reverse = direction == 1
    n, c = xr.shape
    tm = min(tm, seq_len)
    n_tiles = n // tm
    tiles_per_seq = seq_len // tm
    hd = c // N_LRU_HEADS
    fuse_out = other is not None
    blocks_per_tile = tm // SUBLANES
    n_blocks = n // SUBLANES

    def tile(i):
        return (n_tiles - 1 - i) if reverse else i

    row_spec = pl.BlockSpec((tm, c), lambda i: (tile(i), 0))
    vec_spec = pl.BlockSpec((1, c), lambda i: (0, 0))
    w_spec = pl.BlockSpec((None, None, N_LRU_HEADS, hd, hd), lambda i: (layer, direction, 0, 0, 0))
    in_specs = [
        row_spec,
        pl.BlockSpec((SUBLANES, c), lambda i: (jnp.maximum(tile(i) * blocks_per_tile - 1, 0), 0)),
        pl.BlockSpec((SUBLANES, c), lambda i: (jnp.minimum((tile(i) + 1) * blocks_per_tile, n_blocks - 1), 0)),
        pl.BlockSpec((CONV_SHORT, c), lambda i: (0, 0)),
        vec_spec, w_spec, vec_spec, w_spec, vec_spec, vec_spec,
        pl.BlockSpec((1, 1, c), lambda i: (tile(i) // tiles_per_seq, 0, 0)),
    ]
    args = [xr, xr, xr, c4w, c4b.reshape(1, c), wr, br.reshape(1, c), wi, bi.reshape(1, c),
            lam.reshape(1, c), h0]
    scratch = [pltpu.VMEM((tm + 2 * SUBLANES, c), F32), pltpu.VMEM((tm, c), F32),
               pltpu.VMEM((tm, c), F32), pltpu.VMEM((1, c), F32)]
    states = jax.ShapeDtypeStruct((n, c), F32)
    fused = jax.ShapeDtypeStruct((n, c), BF16)
    if fuse_out:
        in_specs += [row_spec, row_spec]
        args += [other, gelu]
    if fuse_out and want_states:
        out_specs, out_shape = [row_spec, row_spec], [fused, states]
    elif fuse_out:
        out_specs, out_shape = row_spec, fused
        scratch = [pltpu.VMEM((tm, c), F32)] + scratch
    else:
        out_specs, out_shape = row_spec, states
    return pl.pallas_call(
        functools.partial(_scan_kernel, reverse=reverse, tiles_per_seq=tiles_per_seq,
                          n_tiles=n_tiles, fuse_out=fuse_out),
        grid=(n_tiles,),
        in_specs=in_specs,
        out_specs=out_specs,
        out_shape=out_shape,
        scratch_shapes=scratch,
        compiler_params=_params(("arbitrary",)),
        name="lru_scan",
    )(*args)


def _mix_out_kernel(x_ref, gt_ref, yr_ref, yc_ref, wa_ref, wb_ref, bo_ref, g_ref, b_ref, o_ref, *, alpha):
    y = jnp.dot(yr_ref[...], wa_ref[...], preferred_element_type=F32)
    y = y + jnp.dot(yc_ref[...], wb_ref[...], preferred_element_type=F32) + bo_ref[...]
    z = alpha * x_ref[...] + gt_ref[0] * y
    o_ref[...] = _layernorm(z, g_ref[...], b_ref[...])


def _mix_out(x, gate, y_rec, y_conv, w_out, b_out, g, b, *, layer, alpha, tm=512):
    n, d = x.shape
    d_lru = y_rec.shape[1]
    d_conv = y_conv.shape[1]
    groups = gate.shape[0]
    tiles_per_group = n // groups // tm
    vec_spec = pl.BlockSpec((1, d), lambda i: (0, 0))
    assert d_lru == d_conv
    return pl.pallas_call(
        functools.partial(_mix_out_kernel, alpha=alpha),
        grid=(n // tm,),
        in_specs=[
            pl.BlockSpec((tm, d), lambda i: (i, 0)),
            pl.BlockSpec((1, 1, d), lambda i: (i // tiles_per_group, 0, 0)),
            pl.BlockSpec((tm, d_lru), lambda i: (i, 0)),
            pl.BlockSpec((tm, d_conv), lambda i: (i, 0)),
            pl.BlockSpec((None, d_lru, d), lambda i: (layer, 0, 0)),
            pl.BlockSpec((None, d_conv, d), lambda i: (layer, 1, 0)),
            vec_spec, vec_spec, vec_spec,
        ],
        out_specs=pl.BlockSpec((tm, d), lambda i: (i, 0)),
        out_shape=jax.ShapeDtypeStruct((n, d), F32),
        compiler_params=_params(("parallel",)),
        name="mix_out",
    )(x, gate, y_rec, y_conv, w_out, w_out, b_out.reshape(1, d), g.reshape(1, d), b.reshape(1, d))


def kernel(x, c, ctx, c_ctx, w_ada, b_ada, ln_g, ln_b, ff1_in, ff1_out, ff2_in, ff2_out, w_in, conv4_w,
           conv4_b, w_rg, b_rg, w_ig, b_ig, lam, conv31_w, conv31_b, cln_g, cln_b, w_out, b_out):
    batch, seq, d = x.shape
    ctx_len = ctx.shape[1]
    depth = w_ada.shape[0]
    d_lru = conv4_w.shape[-1]
    alpha = (2 * depth) ** 0.25

    cond = jnp.concatenate([c, c_ctx[None, :], jnp.zeros((ADA_ROWS - batch - 1, d), F32)], axis=0)
    mods = _ada(cond, w_ada, b_ada).reshape(depth, ADA_ROWS, N_MOD, d)

    w_in_b, w_out_b = w_in.astype(BF16), w_out.astype(BF16)
    wr, wi = w_rg.astype(BF16), w_ig.astype(BF16)

    xs = x.reshape(batch * seq, d)
    cs = ctx.reshape(batch * ctx_len, d)
    zeros_h0 = jnp.zeros((batch, 1, d_lru), F32)
    for l in range(depth):
        last = l == depth - 1
        m = [mods[l, 0:batch, k][:, None, :] for k in range(N_MOD)]
        mc = [mods[l, batch:batch + 1, k][:, None, :] for k in range(N_MOD)]
        conv = (conv31_w[l], conv31_b[l], cln_g[l], cln_b[l])
        ln1, ln2, ln3 = ((ln_g[l, k], ln_b[l, k]) for k in range(3))

        def scan(xr_, h0_, direction, seq_len, **kw):
            return _scan(xr_, conv4_w[l], conv4_b[l], wr, b_rg[l, direction], wi, b_ig[l, direction],
                         lam[l, direction], h0_, layer=l, direction=direction, seq_len=seq_len, **kw)

        cs, ff1_b = _half_ffn(cs, mc[0], mc[1], mc[2], *ln1, alpha=alpha, w_f32=(ff1_in, ff1_out, l))
        if not last:
            xr_c, gl_c, yc_c = _mix_in(cs, mc[3], mc[4], w_in_b, *conv, layer=l, seg=ctx_len, d_lru=d_lru,
                                       conv_out=True)
        else:
            (xr_c,) = _mix_in(cs, mc[3], mc[4], w_in_b, *conv, layer=l, seg=ctx_len, d_lru=d_lru,
                              conv_out=False)
        hb_c = scan(xr_c, zeros_h0, 1, ctx_len)
        if not last:
            yr_c, hf_c = scan(xr_c, zeros_h0, 0, ctx_len, other=hb_c, gelu=gl_c)
            cs = _mix_out(cs, mc[5], yr_c, yc_c, w_out_b, b_out[l], *ln2, layer=l, alpha=alpha)
            cs, ff2_b = _half_ffn(cs, mc[6], mc[7], mc[8], *ln3, alpha=alpha, w_f32=(ff2_in, ff2_out, l))
        else:
            hf_c = scan(xr_c, zeros_h0, 0, ctx_len)
            ff2_b = _cast_ffn_weights(ff2_in, ff2_out, l)
        h0_b = hb_c.reshape(batch, ctx_len, d_lru)[:, 0:1, :]
        h0_f = hf_c.reshape(batch, ctx_len, d_lru)[:, ctx_len - 1:ctx_len, :]

        xs = _half_ffn(xs, m[0], m[1], m[2], *ln1, alpha=alpha, w_bf16=ff1_b)
        xr, gl, yc = _mix_in(xs, m[3], m[4], w_in_b, *conv, layer=l, seg=GRID_W, d_lru=d_lru, conv_out=True)
        hb = scan(xr, h0_b, 1, seq)
        yr = scan(xr, h0_f, 0, seq, other=hb, gelu=gl, want_states=False)
        xs = _mix_out(xs, m[5], yr, yc, w_out_b, b_out[l], *ln2, layer=l, alpha=alpha)
        xs = _half_ffn(xs, m[6], m[7], m[8], *ln3, alpha=alpha, w_bf16=ff2_b)
    return xs.reshape(batch, seq, d)
```

```python
import functools

import jax
import jax.numpy as jnp
from jax import lax
from jax.experimental import pallas as pl
from jax.experimental.pallas import tpu as pltpu

GRID_W = 64
N_LRU_HEADS = 4
RG_C = 8.0
CONV_SHORT = 4
SHORT_PAD_L = 2
CONV_K = 31
N_MOD = 9
ADA_ROWS = 8
EPS = 1e-6

SUBLANES = 8
LANES = 128
CONV_ROWS = 64
CONV_LANES = LANES
FFN_ROWS = 1024
FFN_COLS = 512
FFN_CAST_COLS = 512
FFN_TAIL_ROWS = 256
PROJ_COLS = 256
VMEM_LIMIT = 56 * 1024 * 1024

F32 = jnp.float32
BF16 = jnp.bfloat16


def _params(semantics):
    return pltpu.CompilerParams(dimension_semantics=semantics, vmem_limit_bytes=VMEM_LIMIT)


def _sigmoid(x):
    return 0.5 + 0.5 * jnp.tanh(0.5 * x)


def _layernorm(z, g, b):
    mu = jnp.mean(z, axis=-1, keepdims=True)
    d = z - mu
    var = jnp.mean(d * d, axis=-1, keepdims=True)
    return d * lax.rsqrt(var + EPS) * g + b


def _gelu_tanh(x):
    return 0.5 * x * (1.0 + jnp.tanh(0.7978845608028654 * (x + 0.044715 * (x * x * x))))


def _ada_kernel(c_ref, w_ref, b_ref, o_ref):
    c = c_ref[...]
    a = c * _sigmoid(c)
    o_ref[0] = jnp.dot(a, w_ref[0], preferred_element_type=F32) + b_ref[0]


def _ada(cond, w_ada, b_ada, tn=1024):
    depth, d, n = w_ada.shape
    return pl.pallas_call(
        _ada_kernel,
        grid=(depth, n // tn),
        in_specs=[
            pl.BlockSpec((ADA_ROWS, d), lambda l, j: (0, 0)),
            pl.BlockSpec((1, d, tn), lambda l, j: (l, 0, j)),
            pl.BlockSpec((1, 1, tn), lambda l, j: (l, 0, j)),
        ],
        out_specs=pl.BlockSpec((1, ADA_ROWS, tn), lambda l, j: (l, 0, j)),
        out_shape=jax.ShapeDtypeStruct((depth, ADA_ROWS, n), F32),
        compiler_params=_params(("parallel", "parallel")),
        name="ada",
    )(cond, w_ada, b_ada.reshape(depth, 1, n))


def _ffn_kernel(s_ref, sh_ref, sc_ref, gt_ref, w1g_ref, w1u_ref, w2_ref, g_ref, b_ref, o_ref, *rest,
                alpha, n_chunks, emit_weights):
    if emit_weights:
        w1g_out, w1u_out, w2_out, h_ref = rest
    else:
        (h_ref,) = rest
    j = pl.program_id(1)
    tm = s_ref.shape[0]
    assert n_chunks >= 2

    def weights():
        if not emit_weights:
            return w1g_ref[...], w1u_ref[...], w2_ref[...]
        ws = [ref[...].astype(BF16) for ref in (w1g_ref, w1u_ref, w2_ref)]
        for out_ref, w in zip((w1g_out, w1u_out, w2_out), ws):
            out_ref[...] = w
        return ws

    def chunk(first, last):
        w1g, w1u, w2 = weights()
        if first:
            h = (s_ref[...] * (1.0 + sc_ref[0]) + sh_ref[0]).astype(BF16)
            h_ref[...] = h
        else:
            h = h_ref[...]
        gate = jnp.dot(h, w1g, preferred_element_type=F32)
        up = jnp.dot(h, w1u, preferred_element_type=F32)
        act = (gate * _sigmoid(gate) * up).astype(BF16)
        if first:
            o_ref[...] = jnp.dot(act, w2, preferred_element_type=F32)
        elif not last:
            o_ref[...] += jnp.dot(act, w2, preferred_element_type=F32)
        else:
            for r0 in range(0, tm, FFN_TAIL_ROWS):
                rows = slice(r0, r0 + FFN_TAIL_ROWS)
                y = o_ref[rows, :] + jnp.dot(act[rows], w2, preferred_element_type=F32)
                z = alpha * s_ref[rows, :] + 0.5 * gt_ref[0] * y
                o_ref[rows, :] = _layernorm(z, g_ref[...], b_ref[...])

    pl.when(j == 0)(functools.partial(chunk, True, False))
    pl.when((j > 0) & (j < n_chunks - 1))(functools.partial(chunk, False, False))
    pl.when(j == n_chunks - 1)(functools.partial(chunk, False, True))


def _ffn_weight_specs(d, d_ff, tf, layer):
    n_chunks = d_ff // tf
    if layer is None:
        return [pl.BlockSpec((d, tf), lambda *ij: (0, ij[-1])),
                pl.BlockSpec((d, tf), lambda *ij: (0, ij[-1])),
                pl.BlockSpec((tf, d), lambda *ij: (ij[-1], 0))]
    return [pl.BlockSpec((None, d, tf), lambda *ij: (layer, 0, ij[-1])),
            pl.BlockSpec((None, d, tf), lambda *ij: (layer, 0, ij[-1] + n_chunks)),
            pl.BlockSpec((None, tf, d), lambda *ij: (layer, ij[-1], 0))]


def _ffn_weight_shapes(d, d_ff):
    return [jax.ShapeDtypeStruct((d, d_ff), BF16), jax.ShapeDtypeStruct((d, d_ff), BF16),
            jax.ShapeDtypeStruct((d_ff, d), BF16)]


def _half_ffn(s, shift, scale, gate, g, b, *, alpha, w_bf16=None, w_f32=None):
    n, d = s.shape
    emit = w_f32 is not None
    if emit:
        w1, w2, layer = w_f32
        d_ff, tf, w_args = w2.shape[1], FFN_CAST_COLS, (w1, w1, w2)
    else:
        layer, d_ff, tf, w_args = None, w_bf16[2].shape[0], FFN_COLS, w_bf16
    groups = shift.shape[0]
    tm = min(FFN_ROWS, n // groups)
    tiles_per_group = n // groups // tm
    n_chunks = d_ff // tf
    mod_spec = pl.BlockSpec((1, 1, d), lambda i, j: (i // tiles_per_group, 0, 0))
    vec_spec = pl.BlockSpec((1, d), lambda i, j: (0, 0))
    row_spec = pl.BlockSpec((tm, d), lambda i, j: (i, 0))
    out_specs, out_shape = [row_spec], [jax.ShapeDtypeStruct((n, d), F32)]
    if emit:
        assert n == tm, "each weight chunk must be emitted exactly once"
        out_specs += _ffn_weight_specs(d, d_ff, tf, None)
        out_shape += _ffn_weight_shapes(d, d_ff)
    out = pl.pallas_call(
        functools.partial(_ffn_kernel, alpha=alpha, n_chunks=n_chunks, emit_weights=emit),
        grid=(n // tm, n_chunks),
        in_specs=[row_spec, mod_spec, mod_spec, mod_spec] + _ffn_weight_specs(d, d_ff, tf, layer)
        + [vec_spec, vec_spec],
        out_specs=out_specs,
        out_shape=out_shape,
        scratch_shapes=[pltpu.VMEM((tm, d), BF16)],
        compiler_params=_params(("parallel", "arbitrary")),
        name="half_ffn",
    )(s, shift, scale, gate, *w_args, g.reshape(1, d), b.reshape(1, d))
    return (out[0], tuple(out[1:])) if emit else out[0]


def _cast_kernel(w1g_ref, w1u_ref, w2_ref, w1g_out, w1u_out, w2_out):
    for src_ref, dst_ref in ((w1g_ref, w1g_out), (w1u_ref, w1u_out), (w2_ref, w2_out)):
        dst_ref[...] = src_ref[...].astype(BF16)


def _cast_ffn_weights(w1, w2, layer, tf=512):
    d, d_ff = w1.shape[1], w2.shape[1]
    return tuple(pl.pallas_call(
        _cast_kernel,
        grid=(d_ff // tf,),
        in_specs=_ffn_weight_specs(d, d_ff, tf, layer),
        out_specs=_ffn_weight_specs(d, d_ff, tf, None),
        out_shape=_ffn_weight_shapes(d, d_ff),
        compiler_params=_params(("parallel",)),
        name="cast_ffn_weights",
    )(w1, w1, w2))


def _mix_in_kernel(x_ref, sh_ref, sc_ref, w_ref, *rest, seg, conv_out):
    if conv_out:
        c31w_ref, c31b_ref, cg_ref, cb_ref, xr_ref, gl_ref, yc_ref, upad_ref, conv_ref = rest
    else:
        (xr_ref,) = rest
    d_lru = xr_ref.shape[1]
    h = (x_ref[...] * (1.0 + sc_ref[0]) + sh_ref[0]).astype(BF16)
    if not conv_out:
        xr_ref[...] = jnp.dot(h, w_ref[...], preferred_element_type=F32)
        return
    d_conv = yc_ref.shape[1]
    tm = x_ref.shape[0]
    nseg = tm // seg
    half = CONV_K // 2
    lead = 2 * SUBLANES
    tail = upad_ref.shape[1] - lead - seg

    upad_ref[:, 0:lead, :] = jnp.zeros((nseg, lead, d_conv), F32)
    upad_ref[:, lead + seg:, :] = jnp.zeros((nseg, tail, d_conv), F32)

    def proj(c0):
        return jnp.dot(h, w_ref[:, c0:c0 + PROJ_COLS], preferred_element_type=F32)

    assert d_conv == d_lru
    for c0 in range(0, d_lru, PROJ_COLS):
        u = proj(2 * d_lru + c0) * _sigmoid(proj(2 * d_lru + d_conv + c0))
        upad_ref[:, lead:lead + seg, c0:c0 + PROJ_COLS] = u.reshape(nseg, seg, PROJ_COLS)
        xr_ref[:, c0:c0 + PROJ_COLS] = proj(c0)
        gl_ref[:, c0:c0 + PROJ_COLS] = _gelu_tanh(proj(d_lru + c0))

    offs = [lead - half + k for k in range(CONV_K)]
    slab_rows = CONV_ROWS + lead + tail
    for s in range(nseg):
        for rb in range(seg // CONV_ROWS):
            for cc in range(d_conv // CONV_LANES):
                cs = slice(cc * CONV_LANES, (cc + 1) * CONV_LANES)
                slab = upad_ref[s, rb * CONV_ROWS:rb * CONV_ROWS + slab_rows, cs]
                acc = jnp.broadcast_to(c31b_ref[:, cs], (CONV_ROWS, CONV_LANES))
                for r in range(SUBLANES):
                    rot = slab if r == 0 else pltpu.roll(slab, slab_rows - r, 0)
                    for k in range(CONV_K):
                        if offs[k] % SUBLANES == r:
                            q = offs[k] - r
                            assert q + CONV_ROWS <= slab_rows - r
                            acc = acc + c31w_ref[k:k + 1, cs] * rot[q:q + CONV_ROWS]
                row0 = s * seg + rb * CONV_ROWS
                conv_ref[row0:row0 + CONV_ROWS, cs] = acc
    y = _layernorm(conv_ref[...], cg_ref[...], cb_ref[...])
    yc_ref[...] = (y * _sigmoid(y)).astype(BF16)


def _mix_in(x, shift, scale, w_in, c31w, c31b, cg, cb, *, layer, seg, d_lru, conv_out, tm=512):
    n, d = x.shape
    d_in = w_in.shape[2]
    d_conv = (d_in - 2 * d_lru) // 2
    groups = shift.shape[0]
    tiles_per_group = n // groups // tm
    mod_spec = pl.BlockSpec((1, 1, d), lambda i: (i // tiles_per_group, 0, 0))
    row_lru = pl.BlockSpec((tm, d_lru), lambda i: (i, 0))
    in_specs = [pl.BlockSpec((tm, d), lambda i: (i, 0)), mod_spec, mod_spec]
    args = [x, shift, scale]
    if conv_out:
        in_specs += [
            pl.BlockSpec((None, d, d_in), lambda i: (layer, 0, 0)),
            pl.BlockSpec((CONV_K, d_conv), lambda i: (0, 0)),
            pl.BlockSpec((1, d_conv), lambda i: (0, 0)),
            pl.BlockSpec((1, d_conv), lambda i: (0, 0)),
            pl.BlockSpec((1, d_conv), lambda i: (0, 0)),
        ]
        args += [w_in, c31w, c31b.reshape(1, d_conv), cg.reshape(1, d_conv), cb.reshape(1, d_conv)]
        out_specs = [row_lru, row_lru, pl.BlockSpec((tm, d_conv), lambda i: (i, 0))]
        out_shape = [jax.ShapeDtypeStruct((n, d_lru), F32), jax.ShapeDtypeStruct((n, d_lru), F32),
                     jax.ShapeDtypeStruct((n, d_conv), BF16)]
        pad_rows = 2 * SUBLANES + seg + 2 * SUBLANES
        scratch = [pltpu.VMEM((tm // seg, pad_rows, d_conv), F32), pltpu.VMEM((tm, d_conv), F32)]
    else:
        in_specs += [pl.BlockSpec((None, d, d_lru), lambda i: (layer, 0, 0))]
        args += [w_in]
        out_specs = [row_lru]
        out_shape = [jax.ShapeDtypeStruct((n, d_lru), F32)]
        scratch = []
    return pl.pallas_call(
        functools.partial(_mix_in_kernel, seg=seg, conv_out=conv_out),
        grid=(n // tm,),
        in_specs=in_specs,
        out_specs=out_specs,
        out_shape=out_shape,
        scratch_shapes=scratch,
        compiler_params=_params(("parallel",)),
        name="mix_in",
    )(*args)


def _scan_kernel(xr_ref, xp_ref, xn_ref, c4w_ref, c4b_ref, wr_ref, br_ref, wi_ref, bi_ref,
                 lam_ref, h0_ref, *rest, reverse, tiles_per_seq, n_tiles, fuse_out):
    if fuse_out:
        hb_ref, gl_ref, o_ref, hs_ref, ext_ref, a_ref, b_ref, carry_ref = rest
    else:
        hs_ref, ext_ref, a_ref, b_ref, carry_ref = rest
    i = pl.program_id(0)
    t = (n_tiles - 1 - i) if reverse else i
    p = t % tiles_per_seq
    first = p == 0
    last = p == tiles_per_seq - 1
    tm, c = xr_ref.shape
    hd = c // N_LRU_HEADS

    ext_ref[0:SUBLANES, :] = jnp.where(first, 0.0, xp_ref[...])
    ext_ref[SUBLANES:SUBLANES + tm, :] = xr_ref[...]
    ext_ref[SUBLANES + tm:, :] = jnp.where(last, 0.0, xn_ref[...])
    ext = ext_ref[...]
    xc = jnp.zeros((tm, c), F32) + c4b_ref[...]
    for k in range(CONV_SHORT):
        off = SUBLANES - SHORT_PAD_L + k
        shifted = ext if off % SUBLANES == 0 else pltpu.roll(ext, ext.shape[0] - off % SUBLANES, 0)
        base = off - off % SUBLANES
        xc = xc + c4w_ref[k:k + 1, :] * shifted[base:base + tm, :]

    xcb = xc.astype(BF16)
    lam = lam_ref[...]
    softplus_neg_lam = jnp.maximum(-lam, 0.0) + jnp.log(1.0 + jnp.exp(-jnp.abs(lam)))
    coef = -RG_C * softplus_neg_lam
    for hh in range(N_LRU_HEADS):
        cs = slice(hh * hd, (hh + 1) * hd)
        xh = xcb[:, cs]
        r = _sigmoid(jnp.dot(xh, wr_ref[hh], preferred_element_type=F32) + br_ref[:, cs])
        ig = _sigmoid(jnp.dot(xh, wi_ref[hh], preferred_element_type=F32) + bi_ref[:, cs])
        log_a = coef[:, cs] * r
        a = jnp.exp(log_a)
        a_ref[:, cs] = a
        b_ref[:, cs] = jnp.sqrt(1.0 - a * a) * (ig * xc[:, cs])

    start = last if reverse else first

    @pl.when(start)
    def _():
        carry_ref[...] = h0_ref[0]

    n_groups = tm // SUBLANES
    row = lax.broadcasted_iota(jnp.int32, (SUBLANES, c), 0)

    def group(k, carry):
        kk = (n_groups - 1 - k) if reverse else k
        r0 = pl.multiple_of(kk * SUBLANES, SUBLANES)
        a = a_ref[pl.ds(r0, SUBLANES), :]
        b = b_ref[pl.ds(r0, SUBLANES), :]
        for s in (1, 2, 4):
            if reverse:
                shift, keep = SUBLANES - s, row < SUBLANES - s
            else:
                shift, keep = s, row >= s
            a_prev = jnp.where(keep, pltpu.roll(a, shift, 0), 1.0)
            b_prev = jnp.where(keep, pltpu.roll(b, shift, 0), 0.0)
            b = b + a * b_prev
            a = a * a_prev
        h = b + a * carry
        hs_ref[pl.ds(r0, SUBLANES), :] = h
        return h[0:1, :] if reverse else h[SUBLANES - 1:SUBLANES, :]

    carry_ref[...] = lax.fori_loop(0, n_groups, group, carry_ref[...], unroll=2)

    if fuse_out:
        o_ref[...] = ((hs_ref[...] + hb_ref[...]) * gl_ref[...]).astype(BF16)


def _scan(xr, c4w, c4b, wr, br, wi, bi, lam, h0, *, layer, direction, seq_len, other=None, gelu=None,
          want_states=True, tm=512):
    reverse = direction == 1
    n, c = xr.shape
    tm = min(tm, seq_len)
    n_tiles = n // tm
    tiles_per_seq = seq_len // tm
    hd = c // N_LRU_HEADS
    fuse_out = other is not None
    blocks_per_tile = tm // SUBLANES
    n_blocks = n // SUBLANES

    def tile(i):
        return (n_tiles - 1 - i) if reverse else i

    row_spec = pl.BlockSpec((tm, c), lambda i: (tile(i), 0))
    vec_spec = pl.BlockSpec((1, c), lambda i: (0, 0))
    w_spec = pl.BlockSpec((None, None, N_LRU_HEADS, hd, hd), lambda i: (layer, direction, 0, 0, 0))
    in_specs = [
        row_spec,
        pl.BlockSpec((SUBLANES, c), lambda i: (jnp.maximum(tile(i) * blocks_per_tile - 1, 0), 0)),
        pl.BlockSpec((SUBLANES, c), lambda i: (jnp.minimum((tile(i) + 1) * blocks_per_tile, n_blocks - 1), 0)),
        pl.BlockSpec((CONV_SHORT, c), lambda i: (0, 0)),
        vec_spec, w_spec, vec_spec, w_spec, vec_spec, vec_spec,
        pl.BlockSpec((1, 1, c), lambda i: (tile(i) // tiles_per_seq, 0, 0)),
    ]
    args = [xr, xr, xr, c4w, c4b.reshape(1, c), wr, br.reshape(1, c), wi, bi.reshape(1, c),
            lam.reshape(1, c), h0]
    scratch = [pltpu.VMEM((tm + 2 * SUBLANES, c), F32), pltpu.VMEM((tm, c), F32),
               pltpu.VMEM((tm, c), F32), pltpu.VMEM((1, c), F32)]
    states = jax.ShapeDtypeStruct((n, c), F32)
    fused = jax.ShapeDtypeStruct((n, c), BF16)
    if fuse_out:
        in_specs += [row_spec, row_spec]
        args += [other, gelu]
    if fuse_out and want_states:
        out_specs, out_shape = [row_spec, row_spec], [fused, states]
    elif fuse_out:
        out_specs, out_shape = row_spec, fused
        scratch = [pltpu.VMEM((tm, c), F32)] + scratch
    else:
        out_specs, out_shape = row_spec, states
    return pl.pallas_call(
        functools.partial(_scan_kernel, reverse=reverse, tiles_per_seq=tiles_per_seq,
                          n_tiles=n_tiles, fuse_out=fuse_out),
        grid=(n_tiles,),
        in_specs=in_specs,
        out_specs=out_specs,
        out_shape=out_shape,
        scratch_shapes=scratch,
        compiler_params=_params(("arbitrary",)),
        name="lru_scan",
    )(*args)


def _mix_out_kernel(x_ref, gt_ref, yr_ref, yc_ref, wa_ref, wb_ref, bo_ref, g_ref, b_ref, o_ref, *, alpha):
    y = jnp.dot(yr_ref[...], wa_ref[...], preferred_element_type=F32)
    y = y + jnp.dot(yc_ref[...], wb_ref[...], preferred_element_type=F32) + bo_ref[...]
    z = alpha * x_ref[...] + gt_ref[0] * y
    o_ref[...] = _layernorm(z, g_ref[...], b_ref[...])


def _mix_out(x, gate, y_rec, y_conv, w_out, b_out, g, b, *, layer, alpha, tm=512):
    n, d = x.shape
    d_lru = y_rec.shape[1]
    d_conv = y_conv.shape[1]
    groups = gate.shape[0]
    tiles_per_group = n // groups // tm
    vec_spec = pl.BlockSpec((1, d), lambda i: (0, 0))
    assert d_lru == d_conv
    return pl.pallas_call(
        functools.partial(_mix_out_kernel, alpha=alpha),
        grid=(n // tm,),
        in_specs=[
            pl.BlockSpec((tm, d), lambda i: (i, 0)),
            pl.BlockSpec((1, 1, d), lambda i: (i // tiles_per_group, 0, 0)),
            pl.BlockSpec((tm, d_lru), lambda i: (i, 0)),
            pl.BlockSpec((tm, d_conv), lambda i: (i, 0)),
            pl.BlockSpec((None, d_lru, d), lambda i: (layer, 0, 0)),
            pl.BlockSpec((None, d_conv, d), lambda i: (layer, 1, 0)),
            vec_spec, vec_spec, vec_spec,
        ],
        out_specs=pl.BlockSpec((tm, d), lambda i: (i, 0)),
        out_shape=jax.ShapeDtypeStruct((n, d), F32),
        compiler_params=_params(("parallel",)),
        name="mix_out",
    )(x, gate, y_rec, y_conv, w_out, w_out, b_out.reshape(1, d), g.reshape(1, d), b.reshape(1, d))


def kernel(x, c, ctx, c_ctx, w_ada, b_ada, ln_g, ln_b, ff1_in, ff1_out, ff2_in, ff2_out, w_in, conv4_w,
           conv4_b, w_rg, b_rg, w_ig, b_ig, lam, conv31_w, conv31_b, cln_g, cln_b, w_out, b_out):
    batch, seq, d = x.shape
    ctx_len = ctx.shape[1]
    depth = w_ada.shape[0]
    d_lru = conv4_w.shape[-1]
    alpha = (2 * depth) ** 0.25

    cond = jnp.concatenate([c, c_ctx[None, :], jnp.zeros((ADA_ROWS - batch - 1, d), F32)], axis=0)
    mods = _ada(cond, w_ada, b_ada).reshape(depth, ADA_ROWS, N_MOD, d)

    w_in_b, w_out_b = w_in.astype(BF16), w_out.astype(BF16)
    wr, wi = w_rg.astype(BF16), w_ig.astype(BF16)

    xs = x.reshape(batch * seq, d)
    cs = ctx.reshape(batch * ctx_len, d)
    zeros_h0 = jnp.zeros((batch, 1, d_lru), F32)
    for l in range(depth):
        last = l == depth - 1
        m = [mods[l, 0:batch, k][:, None, :] for k in range(N_MOD)]
        mc = [mods[l, batch:batch + 1, k][:, None, :] for k in range(N_MOD)]
        conv = (conv31_w[l], conv31_b[l], cln_g[l], cln_b[l])
        ln1, ln2, ln3 = ((ln_g[l, k], ln_b[l, k]) for k in range(3))

        def scan(xr_, h0_, direction, seq_len, **kw):
            return _scan(xr_, conv4_w[l], conv4_b[l], wr, b_rg[l, direction], wi, b_ig[l, direction],
                         lam[l, direction], h0_, layer=l, direction=direction, seq_len=seq_len, **kw)

        cs, ff1_b = _half_ffn(cs, mc[0], mc[1], mc[2], *ln1, alpha=alpha, w_f32=(ff1_in, ff1_out, l))
        if not last:
            xr_c, gl_c, yc_c = _mix_in(cs, mc[3], mc[4], w_in_b, *conv, layer=l, seg=ctx_len, d_lru=d_lru,
                                       conv_out=True)
        else:
            (xr_c,) = _mix_in(cs, mc[3], mc[4], w_in_b, *conv, layer=l, seg=ctx_len, d_lru=d_lru,
                              conv_out=False)
        hb_c = scan(xr_c, zeros_h0, 1, ctx_len)
        if not last:
            yr_c, hf_c = scan(xr_c, zeros_h0, 0, ctx_len, other=hb_c, gelu=gl_c)
            cs = _mix_out(cs, mc[5], yr_c, yc_c, w_out_b, b_out[l], *ln2, layer=l, alpha=alpha)
            cs, ff2_b = _half_ffn(cs, mc[6], mc[7], mc[8], *ln3, alpha=alpha, w_f32=(ff2_in, ff2_out, l))
        else:
            hf_c = scan(xr_c, zeros_h0, 0, ctx_len)
            ff2_b = _cast_ffn_weights(ff2_in, ff2_out, l)
        h0_b = hb_c.reshape(batch, ctx_len, d_lru)[:, 0:1, :]
        h0_f = hf_c.reshape(batch, ctx_len, d_lru)[:, ctx_len - 1:ctx_len, :]

        xs = _half_ffn(xs, m[0], m[1], m[2], *ln1, alpha=alpha, w_bf16=ff1_b)
        xr, gl, yc = _mix_in(xs, m[3], m[4], w_in_b, *conv, layer=l, seg=GRID_W, d_lru=d_lru, conv_out=True)
        hb = scan(xr, h0_b, 1, seq)
        yr = scan(xr, h0_f, 0, seq, other=hb, gelu=gl, want_states=False)
        xs = _mix_out(xs, m[5], yr, yc, w_out_b, b_out[l], *ln2, layer=l, alpha=alpha)
        xs = _half_ffn(xs, m[6], m[7], m[8], *ln3, alpha=alpha, w_bf16=ff2_b)
    return xs.reshape(batch, seq, d)
```

```python
import functools

import jax
import jax.numpy as jnp
from jax import lax
from jax.experimental import pallas as pl
from jax.experimental.pallas import tpu as pltpu

GRID_W = 64
N_LRU_HEADS = 4
RG_C = 8.0
CONV_SHORT = 4
SHORT_PAD_L = 2
CONV_K = 31
N_MOD = 9
ADA_ROWS = 8
EPS = 1e-6

SUBLANES = 8
LANES = 128
CONV_ROWS = 64
CONV_LANES = LANES
FFN_ROWS = 1024
FFN_COLS = 512
FFN_CAST_COLS = 256
FFN_TAIL_ROWS = 256
PROJ_COLS = 256
VMEM_LIMIT = 56 * 1024 * 1024

F32 = jnp.float32
BF16 = jnp.bfloat16


def _params(semantics):
    return pltpu.CompilerParams(dimension_semantics=semantics, vmem_limit_bytes=VMEM_LIMIT)


def _sigmoid(x):
    return 0.5 + 0.5 * jnp.tanh(0.5 * x)


def _layernorm(z, g, b):
    mu = jnp.mean(z, axis=-1, keepdims=True)
    d = z - mu
    var = jnp.mean(d * d, axis=-1, keepdims=True)
    return d * lax.rsqrt(var + EPS) * g + b


def _gelu_tanh(x):
    return 0.5 * x * (1.0 + jnp.tanh(0.7978845608028654 * (x + 0.044715 * (x * x * x))))


def _ada_kernel(c_ref, w_ref, b_ref, o_ref):
    c = c_ref[...]
    a = c * _sigmoid(c)
    o_ref[0] = jnp.dot(a, w_ref[0], preferred_element_type=F32) + b_ref[0]


def _ada(cond, w_ada, b_ada, tn=1024):
    depth, d, n = w_ada.shape
    return pl.pallas_call(
        _ada_kernel,
        grid=(depth, n // tn),
        in_specs=[
            pl.BlockSpec((ADA_ROWS, d), lambda l, j: (0, 0)),
            pl.BlockSpec((1, d, tn), lambda l, j: (l, 0, j)),
            pl.BlockSpec((1, 1, tn), lambda l, j: (l, 0, j)),
        ],
        out_specs=pl.BlockSpec((1, ADA_ROWS, tn), lambda l, j: (l, 0, j)),
        out_shape=jax.ShapeDtypeStruct((depth, ADA_ROWS, n), F32),
        compiler_params=_params(("parallel", "parallel")),
        name="ada",
    )(cond, w_ada, b_ada.reshape(depth, 1, n))


def _ffn_kernel(s_ref, sh_ref, sc_ref, gt_ref, w1g_ref, w1u_ref, w2_ref, g_ref, b_ref, o_ref, *rest,
                alpha, n_chunks, emit_weights):
    if emit_weights:
        w1g_out, w1u_out, w2_out, h_ref = rest
    else:
        (h_ref,) = rest
    j = pl.program_id(1)
    tm = s_ref.shape[0]
    assert n_chunks >= 2

    def weights():
        if not emit_weights:
            return w1g_ref[...], w1u_ref[...], w2_ref[...]
        ws = [ref[...].astype(BF16) for ref in (w1g_ref, w1u_ref, w2_ref)]
        for out_ref, w in zip((w1g_out, w1u_out, w2_out), ws):
            out_ref[...] = w
        return ws

    def chunk(first, last):
        w1g, w1u, w2 = weights()
        if first:
            h = (s_ref[...] * (1.0 + sc_ref[0]) + sh_ref[0]).astype(BF16)
            h_ref[...] = h
        else:
            h = h_ref[...]
        gate = jnp.dot(h, w1g, preferred_element_type=F32)
        up = jnp.dot(h, w1u, preferred_element_type=F32)
        act = (gate * _sigmoid(gate) * up).astype(BF16)
        if first:
            o_ref[...] = jnp.dot(act, w2, preferred_element_type=F32)
        elif not last:
            o_ref[...] += jnp.dot(act, w2, preferred_element_type=F32)
        else:
            for r0 in range(0, tm, FFN_TAIL_ROWS):
                rows = slice(r0, r0 + FFN_TAIL_ROWS)
                y = o_ref[rows, :] + jnp.dot(act[rows], w2, preferred_element_type=F32)
                z = alpha * s_ref[rows, :] + 0.5 * gt_ref[0] * y
                o_ref[rows, :] = _layernorm(z, g_ref[...], b_ref[...])

    pl.when(j == 0)(functools.partial(chunk, True, False))
    pl.when((j > 0) & (j < n_chunks - 1))(functools.partial(chunk, False, False))
    pl.when(j == n_chunks - 1)(functools.partial(chunk, False, True))


def _ffn_weight_specs(d, d_ff, tf, layer):
    n_chunks = d_ff // tf
    if layer is None:
        return [pl.BlockSpec((d, tf), lambda *ij: (0, ij[-1])),
                pl.BlockSpec((d, tf), lambda *ij: (0, ij[-1])),
                pl.BlockSpec((tf, d), lambda *ij: (ij[-1], 0))]
    return [pl.BlockSpec((None, d, tf), lambda *ij: (layer, 0, ij[-1])),
            pl.BlockSpec((None, d, tf), lambda *ij: (layer, 0, ij[-1] + n_chunks)),
            pl.BlockSpec((None, tf, d), lambda *ij: (layer, ij[-1], 0))]


def _ffn_weight_shapes(d, d_ff):
    return [jax.ShapeDtypeStruct((d, d_ff), BF16), jax.ShapeDtypeStruct((d, d_ff), BF16),
            jax.ShapeDtypeStruct((d_ff, d), BF16)]


def _half_ffn(s, shift, scale, gate, g, b, *, alpha, w_bf16=None, w_f32=None):
    n, d = s.shape
    emit = w_f32 is not None
    if emit:
        w1, w2, layer = w_f32
        d_ff, tf, w_args = w2.shape[1], FFN_CAST_COLS, (w1, w1, w2)
    else:
        layer, d_ff, tf, w_args = None, w_bf16[2].shape[0], FFN_COLS, w_bf16
    groups = shift.shape[0]
    tm = min(FFN_ROWS, n // groups)
    tiles_per_group = n // groups // tm
    n_chunks = d_ff // tf
    mod_spec = pl.BlockSpec((1, 1, d), lambda i, j: (i // tiles_per_group, 0, 0))
    vec_spec = pl.BlockSpec((1, d), lambda i, j: (0, 0))
    row_spec = pl.BlockSpec((tm, d), lambda i, j: (i, 0))
    out_specs, out_shape = [row_spec], [jax.ShapeDtypeStruct((n, d), F32)]
    if emit:
        assert n == tm, "each weight chunk must be emitted exactly once"
        out_specs += _ffn_weight_specs(d, d_ff, tf, None)
        out_shape += _ffn_weight_shapes(d, d_ff)
    out = pl.pallas_call(
        functools.partial(_ffn_kernel, alpha=alpha, n_chunks=n_chunks, emit_weights=emit),
        grid=(n // tm, n_chunks),
        in_specs=[row_spec, mod_spec, mod_spec, mod_spec] + _ffn_weight_specs(d, d_ff, tf, layer)
        + [vec_spec, vec_spec],
        out_specs=out_specs,
        out_shape=out_shape,
        scratch_shapes=[pltpu.VMEM((tm, d), BF16)],
        compiler_params=_params(("parallel", "arbitrary")),
        name="half_ffn",
    )(s, shift, scale, gate, *w_args, g.reshape(1, d), b.reshape(1, d))
    return (out[0], tuple(out[1:])) if emit else out[0]


def _cast_kernel(w1g_ref, w1u_ref, w2_ref, w1g_out, w1u_out, w2_out):
    for src_ref, dst_ref in ((w1g_ref, w1g_out), (w1u_ref, w1u_out), (w2_ref, w2_out)):
        dst_ref[...] = src_ref[...].astype(BF16)


def _cast_ffn_weights(w1, w2, layer, tf=512):
    d, d_ff = w1.shape[1], w2.shape[1]
    return tuple(pl.pallas_call(
        _cast_kernel,
        grid=(d_ff // tf,),
        in_specs=_ffn_weight_specs(d, d_ff, tf, layer),
        out_specs=_ffn_weight_specs(d, d_ff, tf, None),
        out_shape=_ffn_weight_shapes(d, d_ff),
        compiler_params=_params(("parallel",)),
        name="cast_ffn_weights",
    )(w1, w1, w2))


def _mix_in_kernel(x_ref, sh_ref, sc_ref, w_ref, *rest, seg, conv_out):
    if conv_out:
        c31w_ref, c31b_ref, cg_ref, cb_ref, xr_ref, gl_ref, yc_ref, upad_ref, conv_ref = rest
    else:
        (xr_ref,) = rest
    d_lru = xr_ref.shape[1]
    h = (x_ref[...] * (1.0 + sc_ref[0]) + sh_ref[0]).astype(BF16)
    if not conv_out:
        xr_ref[...] = jnp.dot(h, w_ref[...], preferred_element_type=F32)
        return
    d_conv = yc_ref.shape[1]
    tm = x_ref.shape[0]
    nseg = tm // seg
    half = CONV_K // 2
    lead = 2 * SUBLANES
    tail = upad_ref.shape[1] - lead - seg

    upad_ref[:, 0:lead, :] = jnp.zeros((nseg, lead, d_conv), F32)
    upad_ref[:, lead + seg:, :] = jnp.zeros((nseg, tail, d_conv), F32)

    def proj(c0):
        return jnp.dot(h, w_ref[:, c0:c0 + PROJ_COLS], preferred_element_type=F32)

    assert d_conv == d_lru
    for c0 in range(0, d_lru, PROJ_COLS):
        u = proj(2 * d_lru + c0) * _sigmoid(proj(2 * d_lru + d_conv + c0))
        upad_ref[:, lead:lead + seg, c0:c0 + PROJ_COLS] = u.reshape(nseg, seg, PROJ_COLS)
        xr_ref[:, c0:c0 + PROJ_COLS] = proj(c0)
        gl_ref[:, c0:c0 + PROJ_COLS] = _gelu_tanh(proj(d_lru + c0))

    offs = [lead - half + k for k in range(CONV_K)]
    slab_rows = CONV_ROWS + lead + tail
    for s in range(nseg):
        for rb in range(seg // CONV_ROWS):
            for cc in range(d_conv // CONV_LANES):
                cs = slice(cc * CONV_LANES, (cc + 1) * CONV_LANES)
                slab = upad_ref[s, rb * CONV_ROWS:rb * CONV_ROWS + slab_rows, cs]
                acc = jnp.broadcast_to(c31b_ref[:, cs], (CONV_ROWS, CONV_LANES))
                for r in range(SUBLANES):
                    rot = slab if r == 0 else pltpu.roll(slab, slab_rows - r, 0)
                    for k in range(CONV_K):
                        if offs[k] % SUBLANES == r:
                            q = offs[k] - r
                            assert q + CONV_ROWS <= slab_rows - r
                            acc = acc + c31w_ref[k:k + 1, cs] * rot[q:q + CONV_ROWS]
                row0 = s * seg + rb * CONV_ROWS
                conv_ref[row0:row0 + CONV_ROWS, cs] = acc
    y = _layernorm(conv_ref[...], cg_ref[...], cb_ref[...])
    yc_ref[...] = (y * _sigmoid(y)).astype(BF16)


def _mix_in(x, shift, scale, w_in, c31w, c31b, cg, cb, *, layer, seg, d_lru, conv_out, tm=512):
    n, d = x.shape
    d_in = w_in.shape[2]
    d_conv = (d_in - 2 * d_lru) // 2
    groups = shift.shape[0]
    tiles_per_group = n // groups // tm
    mod_spec = pl.BlockSpec((1, 1, d), lambda i: (i // tiles_per_group, 0, 0))
    row_lru = pl.BlockSpec((tm, d_lru), lambda i: (i, 0))
    in_specs = [pl.BlockSpec((tm, d), lambda i: (i, 0)), mod_spec, mod_spec]
    args = [x, shift, scale]
    if conv_out:
        in_specs += [
            pl.BlockSpec((None, d, d_in), lambda i: (layer, 0, 0)),
            pl.BlockSpec((CONV_K, d_conv), lambda i: (0, 0)),
            pl.BlockSpec((1, d_conv), lambda i: (0, 0)),
            pl.BlockSpec((1, d_conv), lambda i: (0, 0)),
            pl.BlockSpec((1, d_conv), lambda i: (0, 0)),
        ]
        args += [w_in, c31w, c31b.reshape(1, d_conv), cg.reshape(1, d_conv), cb.reshape(1, d_conv)]
        out_specs = [row_lru, row_lru, pl.BlockSpec((tm, d_conv), lambda i: (i, 0))]
        out_shape = [jax.ShapeDtypeStruct((n, d_lru), F32), jax.ShapeDtypeStruct((n, d_lru), F32),
                     jax.ShapeDtypeStruct((n, d_conv), BF16)]
        pad_rows = 2 * SUBLANES + seg + 2 * SUBLANES
        scratch = [pltpu.VMEM((tm // seg, pad_rows, d_conv), F32), pltpu.VMEM((tm, d_conv), F32)]
    else:
        in_specs += [pl.BlockSpec((None, d, d_lru), lambda i: (layer, 0, 0))]
        args += [w_in]
        out_specs = [row_lru]
        out_shape = [jax.ShapeDtypeStruct((n, d_lru), F32)]
        scratch = []
    return pl.pallas_call(
        functools.partial(_mix_in_kernel, seg=seg, conv_out=conv_out),
        grid=(n // tm,),
        in_specs=in_specs,
        out_specs=out_specs,
        out_shape=out_shape,
        scratch_shapes=scratch,
        compiler_params=_params(("parallel",)),
        name="mix_in",
    )(*args)


def _scan_kernel(xr_ref, xp_ref, xn_ref, c4w_ref, c4b_ref, wr_ref, br_ref, wi_ref, bi_ref,
                 lam_ref, h0_ref, *rest, reverse, tiles_per_seq, n_tiles, fuse_out):
    if fuse_out:
        hb_ref, gl_ref, o_ref, hs_ref, ext_ref, a_ref, b_ref, carry_ref = rest
    else:
        hs_ref, ext_ref, a_ref, b_ref, carry_ref = rest
    i = pl.program_id(0)
    t = (n_tiles - 1 - i) if reverse else i
    p = t % tiles_per_seq
    first = p == 0
    last = p == tiles_per_seq - 1
    tm, c = xr_ref.shape
    hd = c // N_LRU_HEADS

    ext_ref[0:SUBLANES, :] = jnp.where(first, 0.0, xp_ref[...])
    ext_ref[SUBLANES:SUBLANES + tm, :] = xr_ref[...]
    ext_ref[SUBLANES + tm:, :] = jnp.where(last, 0.0, xn_ref[...])
    ext = ext_ref[...]
    xc = jnp.zeros((tm, c), F32) + c4b_ref[...]
    for k in range(CONV_SHORT):
        off = SUBLANES - SHORT_PAD_L + k
        shifted = ext if off % SUBLANES == 0 else pltpu.roll(ext, ext.shape[0] - off % SUBLANES, 0)
        base = off - off % SUBLANES
        xc = xc + c4w_ref[k:k + 1, :] * shifted[base:base + tm, :]

    xcb = xc.astype(BF16)
    lam = lam_ref[...]
    softplus_neg_lam = jnp.maximum(-lam, 0.0) + jnp.log(1.0 + jnp.exp(-jnp.abs(lam)))
    coef = -RG_C * softplus_neg_lam
    for hh in range(N_LRU_HEADS):
        cs = slice(hh * hd, (hh + 1) * hd)
        xh = xcb[:, cs]
        r = _sigmoid(jnp.dot(xh, wr_ref[hh], preferred_element_type=F32) + br_ref[:, cs])
        ig = _sigmoid(jnp.dot(xh, wi_ref[hh], preferred_element_type=F32) + bi_ref[:, cs])
        log_a = coef[:, cs] * r
        a = jnp.exp(log_a)
        a_ref[:, cs] = a
        b_ref[:, cs] = jnp.sqrt(1.0 - a * a) * (ig * xc[:, cs])

    start = last if reverse else first

    @pl.when(start)
    def _():
        carry_ref[...] = h0_ref[0]

    n_groups = tm // SUBLANES
    row = lax.broadcasted_iota(jnp.int32, (SUBLANES, c), 0)

    def group(k, carry):
        kk = (n_groups - 1 - k) if reverse else k
        r0 = pl.multiple_of(kk * SUBLANES, SUBLANES)
        a = a_ref[pl.ds(r0, SUBLANES), :]
        b = b_ref[pl.ds(r0, SUBLANES), :]
        for s in (1, 2, 4):
            if reverse:
                shift, keep = SUBLANES - s, row < SUBLANES - s
            else:
                shift, keep = s, row >= s
            a_prev = jnp.where(keep, pltpu.roll(a, shift, 0), 1.0)
            b_prev = jnp.where(keep, pltpu.roll(b, shift, 0), 0.0)
            b = b + a * b_prev
            a = a * a_prev
        h = b + a * carry
        hs_ref[pl.ds(r0, SUBLANES), :] = h
        return h[0:1, :] if reverse else h[SUBLANES - 1:SUBLANES, :]

    carry_ref[...] = lax.fori_loop(0, n_groups, group, carry_ref[...], unroll=2)

    if fuse_out:
        o_ref[...] = ((hs_ref[...] + hb_ref[...]) * gl_ref[...]).astype(BF16)


def _scan(xr, c4w, c4b, wr, br, wi, bi, lam, h0, *, layer, direction, seq_len, other=None, gelu=None,
          want_states=True, tm=1024):
    reverse = direction == 1
    n, c = xr.shape
    tm = min(tm, seq_len)
    n_tiles = n // tm
    tiles_per_seq = seq_len // tm
    hd = c // N_LRU_HEADS
    fuse_out = other is not None
    blocks_per_tile = tm // SUBLANES
    n_blocks = n // SUBLANES

    def tile(i):
        return (n_tiles - 1 - i) if reverse else i

    row_spec = pl.BlockSpec((tm, c), lambda i: (tile(i), 0))
    vec_spec = pl.BlockSpec((1, c), lambda i: (0, 0))
    w_spec = pl.BlockSpec((None, None, N_LRU_HEADS, hd, hd), lambda i: (layer, direction, 0, 0, 0))
    in_specs = [
        row_spec,
        pl.BlockSpec((SUBLANES, c), lambda i: (jnp.maximum(tile(i) * blocks_per_tile - 1, 0), 0)),
        pl.BlockSpec((SUBLANES, c), lambda i: (jnp.minimum((tile(i) + 1) * blocks_per_tile, n_blocks - 1), 0)),
        pl.BlockSpec((CONV_SHORT, c), lambda i: (0, 0)),
        vec_spec, w_spec, vec_spec, w_spec, vec_spec, vec_spec,
        pl.BlockSpec((1, 1, c), lambda i: (tile(i) // tiles_per_seq, 0, 0)),
    ]
    args = [xr, xr, xr, c4w, c4b.reshape(1, c), wr, br.reshape(1, c), wi, bi.reshape(1, c),
            lam.reshape(1, c), h0]
    scratch = [pltpu.VMEM((tm + 2 * SUBLANES, c), F32), pltpu.VMEM((tm, c), F32),
               pltpu.VMEM((tm, c), F32), pltpu.VMEM((1, c), F32)]
    states = jax.ShapeDtypeStruct((n, c), F32)
    fused = jax.ShapeDtypeStruct((n, c), BF16)
    if fuse_out:
        in_specs += [row_spec, row_spec]
        args += [other, gelu]
    if fuse_out and want_states:
        out_specs, out_shape = [row_spec, row_spec], [fused, states]
    elif fuse_out:
        out_specs, out_shape = row_spec, fused
        scratch = [pltpu.VMEM((tm, c), F32)] + scratch
    else:
        out_specs, out_shape = row_spec, states
    return pl.pallas_call(
        functools.partial(_scan_kernel, reverse=reverse, tiles_per_seq=tiles_per_seq,
                          n_tiles=n_tiles, fuse_out=fuse_out),
        grid=(n_tiles,),
        in_specs=in_specs,
        out_specs=out_specs,
        out_shape=out_shape,
        scratch_shapes=scratch,
        compiler_params=_params(("arbitrary",)),
        name="lru_scan",
    )(*args)


def _mix_out_kernel(x_ref, gt_ref, yr_ref, yc_ref, wa_ref, wb_ref, bo_ref, g_ref, b_ref, o_ref, *, alpha):
    y = jnp.dot(yr_ref[...], wa_ref[...], preferred_element_type=F32)
    y = y + jnp.dot(yc_ref[...], wb_ref[...], preferred_element_type=F32) + bo_ref[...]
    z = alpha * x_ref[...] + gt_ref[0] * y
    o_ref[...] = _layernorm(z, g_ref[...], b_ref[...])


def _mix_out(x, gate, y_rec, y_conv, w_out, b_out, g, b, *, layer, alpha, tm=512):
    n, d = x.shape
    d_lru = y_rec.shape[1]
    d_conv = y_conv.shape[1]
    groups = gate.shape[0]
    tiles_per_group = n // groups // tm
    vec_spec = pl.BlockSpec((1, d), lambda i: (0, 0))
    assert d_lru == d_conv
    return pl.pallas_call(
        functools.partial(_mix_out_kernel, alpha=alpha),
        grid=(n // tm,),
        in_specs=[
            pl.BlockSpec((tm, d), lambda i: (i, 0)),
            pl.BlockSpec((1, 1, d), lambda i: (i // tiles_per_group, 0, 0)),
            pl.BlockSpec((tm, d_lru), lambda i: (i, 0)),
            pl.BlockSpec((tm, d_conv), lambda i: (i, 0)),
            pl.BlockSpec((None, d_lru, d), lambda i: (layer, 0, 0)),
            pl.BlockSpec((None, d_conv, d), lambda i: (layer, 1, 0)),
            vec_spec, vec_spec, vec_spec,
        ],
        out_specs=pl.BlockSpec((tm, d), lambda i: (i, 0)),
        out_shape=jax.ShapeDtypeStruct((n, d), F32),
        compiler_params=_params(("parallel",)),
        name="mix_out",
    )(x, gate, y_rec, y_conv, w_out, w_out, b_out.reshape(1, d), g.reshape(1, d), b.reshape(1, d))


def kernel(x, c, ctx, c_ctx, w_ada, b_ada, ln_g, ln_b, ff1_in, ff1_out, ff2_in, ff2_out, w_in, conv4_w,
           conv4_b, w_rg, b_rg, w_ig, b_ig, lam, conv31_w, conv31_b, cln_g, cln_b, w_out, b_out):
    batch, seq, d = x.shape
    ctx_len = ctx.shape[1]
    depth = w_ada.shape[0]
    d_lru = conv4_w.shape[-1]
    alpha = (2 * depth) ** 0.25

    cond = jnp.concatenate([c, c_ctx[None, :], jnp.zeros((ADA_ROWS - batch - 1, d), F32)], axis=0)
    mods = _ada(cond, w_ada, b_ada).reshape(depth, ADA_ROWS, N_MOD, d)

    w_in_b, w_out_b = w_in.astype(BF16), w_out.astype(BF16)
    wr, wi = w_rg.astype(BF16), w_ig.astype(BF16)

    xs = x.reshape(batch * seq, d)
    cs = ctx.reshape(batch * ctx_len, d)
    zeros_h0 = jnp.zeros((batch, 1, d_lru), F32)
    for l in range(depth):
        last = l == depth - 1
        m = [mods[l, 0:batch, k][:, None, :] for k in range(N_MOD)]
        mc = [mods[l, batch:batch + 1, k][:, None, :] for k in range(N_MOD)]
        conv = (conv31_w[l], conv31_b[l], cln_g[l], cln_b[l])
        ln1, ln2, ln3 = ((ln_g[l, k], ln_b[l, k]) for k in range(3))

        def scan(xr_, h0_, direction, seq_len, **kw):
            return _scan(xr_, conv4_w[l], conv4_b[l], wr, b_rg[l, direction], wi, b_ig[l, direction],
                         lam[l, direction], h0_, layer=l, direction=direction, seq_len=seq_len, **kw)

        cs, ff1_b = _half_ffn(cs, mc[0], mc[1], mc[2], *ln1, alpha=alpha, w_f32=(ff1_in, ff1_out, l))
        if not last:
            xr_c, gl_c, yc_c = _mix_in(cs, mc[3], mc[4], w_in_b, *conv, layer=l, seg=ctx_len, d_lru=d_lru,
                                       conv_out=True)
        else:
            (xr_c,) = _mix_in(cs, mc[3], mc[4], w_in_b, *conv, layer=l, seg=ctx_len, d_lru=d_lru,
                              conv_out=False)
        hb_c = scan(xr_c, zeros_h0, 1, ctx_len)
        if not last:
            yr_c, hf_c = scan(xr_c, zeros_h0, 0, ctx_len, other=hb_c, gelu=gl_c)
            cs = _mix_out(cs, mc[5], yr_c, yc_c, w_out_b, b_out[l], *ln2, layer=l, alpha=alpha)
            cs, ff2_b = _half_ffn(cs, mc[6], mc[7], mc[8], *ln3, alpha=alpha, w_f32=(ff2_in, ff2_out, l))
        else:
            hf_c = scan(xr_c, zeros_h0, 0, ctx_len)
            ff2_b = _cast_ffn_weights(ff2_in, ff2_out, l)
        h0_b = hb_c.reshape(batch, ctx_len, d_lru)[:, 0:1, :]
        h0_f = hf_c.reshape(batch, ctx_len, d_lru)[:, ctx_len - 1:ctx_len, :]

        xs = _half_ffn(xs, m[0], m[1], m[2], *ln1, alpha=alpha, w_bf16=ff1_b)
        xr, gl, yc = _mix_in(xs, m[3], m[4], w_in_b, *conv, layer=l, seg=GRID_W, d_lru=d_lru, conv_out=True)
        hb = scan(xr, h0_b, 1, seq)
        yr = scan(xr, h0_f, 0, seq, other=hb, gelu=gl, want_states=False)
        xs = _mix_out(xs, m[5], yr, yc, w_out_b, b_out[l], *ln2, layer=l, alpha=alpha)
        xs = _half_ffn(xs, m[6], m[7], m[8], *ln3, alpha=alpha, w_bf16=ff2_b)
    return xs.reshape(batch, seq, d)
```
